```python
import jax, jax.numpy as jnp
from jax import lax
import numpy as np

D_MODEL = 2048
BATCH = 16
SEQ = 256
DEPTH = 4
DEC_BATCH = 4
DEC_SEQ = 2048
PAST_LEN = 256

GRID_W = 64
D_MIX = D_MODEL
D_FOURIER = D_MIX // 4
N_FOURIER_GROUPS = 4
FOURIER_GROUP = D_FOURIER // N_FOURIER_GROUPS
D_DELTA = D_MIX - D_FOURIER
HEAD_DIM = 128
N_HEADS = D_DELTA // HEAD_DIM
N_DIR = 2
CONV_K = 3
CHUNK = 64
D_IN = 2 * D_FOURIER + 4 * D_DELTA + 2 * N_DIR * N_HEADS
EPS = 1e-6

kernel_name = "hybrid_fourier_gdn_diffusion_step"


def _rms_norm(x, w):
    xf = x.astype(jnp.float32)
    y = xf * lax.rsqrt(jnp.mean(xf * xf, axis=-1, keepdims=True) + EPS)
    return (y * w.astype(jnp.float32)).astype(x.dtype)


def _l2norm(x):
    return x * lax.rsqrt(jnp.sum(x * x, axis=-1, keepdims=True) + EPS)


def _short_conv(x, w):
    pad = CONV_K // 2
    l = x.shape[1]
    xp = jnp.pad(x, ((0, 0), (pad, pad), (0, 0)))
    y = xp[:, 0:l] * w[0]
    for j in range(1, CONV_K):
        y = y + xp[:, j:j + l] * w[j]
    return jax.nn.silu(y)


def _fourier_mix(u, grid):
    b, l, _ = u.shape
    uf = u.astype(jnp.float32)
    if grid:
        rows = l // GRID_W
        uf = uf.reshape(b, rows, GRID_W, N_FOURIER_GROUPS, FOURIER_GROUP)
        y = jnp.fft.fftn(uf, axes=(1, 2, 4), norm="ortho").real
    else:
        uf = uf.reshape(b, l, N_FOURIER_GROUPS, FOURIER_GROUP)
        y = jnp.fft.fftn(uf, axes=(1, 3), norm="ortho").real
    return y.reshape(b, l, D_FOURIER).astype(u.dtype)


def _chunk_gated_delta(q, k, v, log_g, beta, s0):
    b, l, h, _ = q.shape
    dv = v.shape[-1]
    n = l // CHUNK

    def heads_first(t):
        t = t.reshape((b, n, CHUNK, h) + t.shape[3:])
        return jnp.moveaxis(t, 3, 1)

    q, k, v, log_g, beta = (heads_first(t) for t in (q, k, v, log_g, beta))
    g = jnp.cumsum(log_g, axis=-1)
    idx = jnp.arange(CHUNK)
    incl = idx[:, None] >= idx[None, :]
    strict = idx[:, None] > idx[None, :]
    diff = g[..., :, None] - g[..., None, :]
    decay = jnp.where(incl, jnp.exp(jnp.where(incl, diff, 0.0)), 0.0)
    kb = k * beta[..., None]
    vb = v * beta[..., None]
    lmat = jnp.where(strict, jnp.einsum("bhncd,bhnsd->bhncs", kb, k) * decay, 0.0)
    eye = jnp.eye(CHUNK, dtype=q.dtype)
    a = lmat + eye
    t_inv = lax.linalg.triangular_solve(a, jnp.broadcast_to(eye, a.shape),
                                        left_side=True, lower=True)
    u = jnp.einsum("bhncs,bhnsd->bhncd", t_inv, vb)
    w = jnp.einsum("bhncs,bhnsd->bhncd", t_inv, kb * jnp.exp(g)[..., None])
    attn = jnp.where(incl, jnp.einsum("bhncd,bhnsd->bhncs", q, k) * decay, 0.0)
    qg = q * jnp.exp(g)[..., None]
    g_last = g[..., -1]
    kd = k * jnp.exp(g_last[..., None] - g)[..., None]
    xs = tuple(jnp.moveaxis(t, 2, 0) for t in (u, w, attn, qg, kd, g_last))

    def step(s, inp):
        u_i, w_i, a_i, qg_i, kd_i, gl_i = inp
        v_new = u_i - jnp.einsum("bhcd,bhde->bhce", w_i, s)
        o_i = (jnp.einsum("bhcd,bhde->bhce", qg_i, s)
               + jnp.einsum("bhcs,bhse->bhce", a_i, v_new))
        s = s * jnp.exp(gl_i)[..., None, None] + jnp.einsum("bhcd,bhce->bhde", kd_i, v_new)
        return s, o_i

    s_fin, o = lax.scan(step, s0, xs)
    o = jnp.transpose(o, (1, 0, 3, 2, 4)).reshape(b, l, h, dv)
    return o, s_fin


def _layer(x, mod, norm_w, w_in, conv_w, a_log, dt_bias, gnorm_w, w_out, s0, grid):
    bsz, l, _ = x.shape
    shift, scale, gate = jnp.split(mod, 3, axis=-1)
    h = _rms_norm(x, norm_w) * (1.0 + scale[:, None]) + shift[:, None]
    proj = h @ w_in
    u_f, z_f, qkv, z_d, ab = jnp.split(
        proj, [D_FOURIER, 2 * D_FOURIER, 2 * D_FOURIER + 3 * D_DELTA,
               2 * D_FOURIER + 4 * D_DELTA], axis=-1)
    y_f = _fourier_mix(u_f, grid) * jax.nn.silu(z_f)
    qkv = _short_conv(qkv, conv_w).astype(jnp.float32)
    q, k, v = jnp.split(qkv, 3, axis=-1)
    q = _l2norm(q.reshape(bsz, l, N_HEADS, HEAD_DIM)) * (HEAD_DIM ** -0.5)
    k = _l2norm(k.reshape(bsz, l, N_HEADS, HEAD_DIM))
    v = v.reshape(bsz, l, N_HEADS, HEAD_DIM)
    ab = ab.astype(jnp.float32).reshape(bsz, l, 2, N_DIR, N_HEADS)
    log_g = -jnp.exp(a_log.astype(jnp.float32)) * jax.nn.softplus(
        ab[:, :, 0] + dt_bias.astype(jnp.float32))
    beta = jax.nn.sigmoid(ab[:, :, 1])
    s0 = s0.astype(jnp.float32)
    o_fw, s_fw = _chunk_gated_delta(q, k, v, log_g[:, :, 0], beta[:, :, 0], s0[:, 0])
    flip = lambda t: jnp.flip(t, axis=1)
    o_bw, s_bw = _chunk_gated_delta(flip(q), flip(k), flip(v), flip(log_g[:, :, 1]),
                                    flip(beta[:, :, 1]), s0[:, 1])
    o = (o_fw + flip(o_bw)).astype(x.dtype)
    y_d = _rms_norm(o, gnorm_w).reshape(bsz, l, D_DELTA) * jax.nn.silu(z_d)
    y = jnp.concatenate([y_f, y_d], axis=-1) @ w_out
    return x + gate[:, None] * y, jnp.stack([s_fw, s_bw], axis=1)


def setup_inputs(seed: int = 0) -> dict:
    key = jax.random.key(seed)
    ks = jax.random.split(key, 16)
    f32 = jnp.float32
    x_prompt = jax.random.normal(ks[0], (BATCH, SEQ, D_MODEL), f32)
    x_sample = jax.random.normal(ks[1], (DEC_BATCH, DEC_SEQ, D_MODEL), f32)
    state_ctx = 0.1 * jax.random.normal(
        ks[2], (DEC_BATCH, DEPTH, N_DIR, N_HEADS, HEAD_DIM, HEAD_DIM), f32)
    c = jax.random.normal(ks[3], (DEC_BATCH, D_MODEL), f32)
    c_ctx = jax.random.normal(ks[4], (D_MODEL,), f32)
    norm_w = 1.0 + 0.02 * jax.random.normal(ks[5], (DEPTH, D_MODEL), f32)
    w_mod = 0.5 * D_MODEL ** -0.5 * jax.random.normal(ks[6], (DEPTH, D_MODEL, 3 * D_MODEL), f32)
    b_mod = 0.02 * jax.random.normal(ks[7], (DEPTH, 3 * D_MODEL), f32)
    w_in = D_MODEL ** -0.5 * jax.random.normal(ks[8], (DEPTH, D_MODEL, D_IN), f32)
    conv_w = CONV_K ** -0.5 * jax.random.normal(ks[9], (DEPTH, CONV_K, 3 * D_DELTA), f32)
    a_log = jnp.log(jax.random.uniform(ks[10], (DEPTH, N_DIR, N_HEADS), f32, 1.0, 16.0))
    dt_bias = 0.5 * jax.random.normal(ks[11], (DEPTH, N_DIR, N_HEADS), f32)
    gnorm_w = 1.0 + 0.02 * jax.random.normal(ks[12], (DEPTH, HEAD_DIM), f32)
    w_out = D_MIX ** -0.5 * jax.random.normal(ks[13], (DEPTH, D_MIX, D_MODEL), f32)
    final_norm_w = 1.0 + 0.02 * jax.random.normal(ks[14], (D_MODEL,), f32)
    return {"x_prompt": x_prompt, "x_sample": x_sample, "state_ctx": state_ctx,
            "c": c, "c_ctx": c_ctx, "norm_w": norm_w, "w_mod": w_mod, "b_mod": b_mod,
            "w_in": w_in, "conv_w": conv_w, "a_log": a_log, "dt_bias": dt_bias,
            "gnorm_w": gnorm_w, "w_out": w_out, "final_norm_w": final_norm_w}


def reference(x_prompt, x_sample, state_ctx, c, c_ctx, norm_w, w_mod, b_mod, w_in,
              conv_w, a_log, dt_bias, gnorm_w, w_out, final_norm_w):
    xp = x_prompt
    xs = x_sample
    s_zero = jnp.zeros((x_prompt.shape[0], N_DIR, N_HEADS, HEAD_DIM, HEAD_DIM), jnp.float32)
    silu_ctx = jax.nn.silu(c_ctx)[None]
    silu_c = jax.nn.silu(c)
    states = []
    for i in range(DEPTH):
        mod_ctx = silu_ctx @ w_mod[i] + b_mod[i]
        mod_lat = silu_c @ w_mod[i] + b_mod[i]
        xp, s_new = _layer(xp, mod_ctx, norm_w[i], w_in[i], conv_w[i], a_log[i],
                           dt_bias[i], gnorm_w[i], w_out[i], s_zero, False)
        states.append(s_new)
        xs, _ = _layer(xs, mod_lat, norm_w[i], w_in[i], conv_w[i], a_log[i],
                       dt_bias[i], gnorm_w[i], w_out[i], state_ctx[:, i], True)
    y_prompt = _rms_norm(xp, final_norm_w)
    y_sample = _rms_norm(xs, final_norm_w)
    state_new = jnp.stack(states, axis=1).astype(x_prompt.dtype)
    return (y_prompt, y_sample, state_new)
```

```python
import functools
import math

import numpy as np
import jax
import jax.numpy as jnp
from jax import lax
from jax.experimental import pallas as pl
from jax.experimental.pallas import tpu as pltpu

D_MODEL = 2048
GRID_W = 64
D_FOURIER = 512
N_FOURIER_GROUPS = 4
FOURIER_GROUP = 128
D_DELTA = 1536
HEAD_DIM = 128
N_HEADS = 12
N_DIR = 2
CONV_K = 3
CHUNK = 64
EPS = 1e-6

PAIR = 2 * CHUNK
D_MAIN = 2 * D_FOURIER + 4 * D_DELTA
N_GATE = 2 * N_DIR * N_HEADS
LANES = 128
COL_Q = (2 * D_FOURIER) // LANES
COL_K = COL_Q + N_HEADS
COL_V = COL_K + N_HEADS
COL_ZD = COL_V + N_HEADS
VMEM_LIMIT = 56 * 1024 * 1024

BF16 = jnp.bfloat16
F32 = jnp.float32
NT_DIMS = (((1,), (1,)), ((), ()))


def _mm(a, b):
    return jnp.dot(a.astype(BF16), b.astype(BF16), preferred_element_type=F32)


def _silu(x):
    return x / (1.0 + jnp.exp(-x))


def _mod_kernel(c_ref, w_ref, b_ref, o_ref):
    s = _silu(c_ref[...])
    o_ref[0] = _mm(s, w_ref[0]) + b_ref[0]


def _modulation(cvec, w_mod, b_mod):
    depth, _, n = w_mod.shape
    tn = 512
    return pl.pallas_call(
        _mod_kernel,
        grid=(depth, n // tn),
        in_specs=[pl.BlockSpec((8, D_MODEL), lambda l, j: (0, 0)),
                  pl.BlockSpec((1, D_MODEL, tn), lambda l, j: (l, 0, j)),
                  pl.BlockSpec((1, 1, tn), lambda l, j: (l, 0, j))],
        out_specs=pl.BlockSpec((1, 8, tn), lambda l, j: (l, 0, j)),
        out_shape=jax.ShapeDtypeStruct((depth, 8, n), F32),
        compiler_params=pltpu.CompilerParams(dimension_semantics=("arbitrary", "arbitrary")),
    )(cvec, w_mod, b_mod.reshape(depth, 1, n))


def _chunk_scans(lg, axis):
    n = lg.shape[axis]
    pos = lax.broadcasted_iota(jnp.int32, lg.shape, axis) % CHUNK
    pre, suf = lg, lg
    s = 1
    while s < CHUNK:
        pre = pre + jnp.where(pos >= s, pltpu.roll(pre, s, axis), 0.0)
        suf = suf + jnp.where(pos < CHUNK - s, pltpu.roll(suf, n - s, axis), 0.0)
        s *= 2
    return pre, suf


def _gate_tables(ab, a_log, dt_bias, gate_axis):
    tok_axis = 1 - gate_axis
    col = lax.broadcasted_iota(jnp.int32, ab.shape, gate_axis)
    xa = ab + dt_bias
    softplus = jnp.maximum(xa, 0.0) + jnp.log1p(jnp.exp(-jnp.abs(xa)))
    lg = jnp.where(col < N_DIR * N_HEADS, -jnp.exp(a_log) * softplus, 0.0)
    beta = 1.0 / (1.0 + jnp.exp(-ab))
    pre, suf = _chunk_scans(lg, tok_axis)
    fwd = col < N_HEADS
    dec = col < N_DIR * N_HEADS
    incl = jnp.where(fwd, pre, jnp.where(dec, suf, jnp.where(col < N_GATE, beta, 0.0)))
    rest = jnp.where(fwd, suf - lg, jnp.where(dec, pre - lg, 0.0))
    return incl, rest


def _inproj_kernel(x_ref, shift_ref, scale_ref, nw_ref, w_ref, wab_ref, wabt_ref,
                   alog_r_ref, dt_r_ref, alog_c_ref, dt_c_ref,
                   proj_ref, gi_ref, gr_ref, git_ref, h_scr, *, slab):
    j = pl.program_id(1)

    @pl.when(j == 0)
    def _():
        tm = x_ref.shape[0]
        for s in range(tm // slab):
            rows = pl.ds(s * slab, slab)
            xf = x_ref[rows, :]
            ms = jnp.mean(xf * xf, axis=-1, keepdims=True)
            y = xf * lax.rsqrt(ms + EPS) * nw_ref[...]
            hb = (y * (1.0 + scale_ref[0]) + shift_ref[0]).astype(BF16)
            h_scr[rows, :] = hb
            ab = jnp.dot(hb, wab_ref[...], preferred_element_type=F32)
            gi, gr = _gate_tables(ab, alog_r_ref[...], dt_r_ref[...], 1)
            gi_ref[rows, :] = gi
            gr_ref[rows, :] = gr
            abt = lax.dot_general(wabt_ref[...], hb, NT_DIMS, preferred_element_type=F32)
            git, _ = _gate_tables(abt, alog_c_ref[...], dt_c_ref[...], 0)
            git_ref[:, s * slab:(s + 1) * slab] = git

    proj_ref[...] = jnp.dot(h_scr[...], w_ref[...], preferred_element_type=F32)


def _in_projection(x2, mods, mod_row, norm_w, w_bf, wab, wabt, alog_r, dt_r, alog_c, dt_c, seq_len):
    m = x2.shape[0]
    tm, tn, slab = 1024, 512, 256
    assert m % tm == 0 and D_MAIN % tn == 0 and (seq_len % tm == 0 or tm % seq_len == 0)
    if mod_row is None:
        row = lambda i: (i * tm) // seq_len
    else:
        row = lambda i: mod_row
    const = lambda i, j: (0, 0)
    return pl.pallas_call(
        functools.partial(_inproj_kernel, slab=slab),
        grid=(m // tm, D_MAIN // tn),
        in_specs=[pl.BlockSpec((tm, D_MODEL), lambda i, j: (i, 0)),
                  pl.BlockSpec((1, 1, D_MODEL), lambda i, j: (row(i), 0, 0)),
                  pl.BlockSpec((1, 1, D_MODEL), lambda i, j: (row(i), 0, 1)),
                  pl.BlockSpec((1, D_MODEL), const),
                  pl.BlockSpec((D_MODEL, tn), lambda i, j: (0, j)),
                  pl.BlockSpec((D_MODEL, LANES), const),
                  pl.BlockSpec((LANES, D_MODEL), const),
                  pl.BlockSpec((1, LANES), const),
                  pl.BlockSpec((1, LANES), const),
                  pl.BlockSpec((LANES, 1), const),
                  pl.BlockSpec((LANES, 1), const)],
        out_specs=[pl.BlockSpec((tm, tn), lambda i, j: (i, j)),
                   pl.BlockSpec((tm, LANES), lambda i, j: (i, 0)),
                   pl.BlockSpec((tm, LANES), lambda i, j: (i, 0)),
                   pl.BlockSpec((LANES, tm), lambda i, j: (0, i))],
        out_shape=[jax.ShapeDtypeStruct((m, D_MAIN), F32),
                   jax.ShapeDtypeStruct((m, LANES), F32),
                   jax.ShapeDtypeStruct((m, LANES), F32),
                   jax.ShapeDtypeStruct((LANES, m), F32)],
        scratch_shapes=[pltpu.VMEM((tm, D_MODEL), BF16)],
        compiler_params=pltpu.CompilerParams(dimension_semantics=("arbitrary", "arbitrary"),
                                             vmem_limit_bytes=VMEM_LIMIT),
    )(x2, mods, mods, norm_w, w_bf, wab, wabt, alog_r, dt_r, alog_c, dt_c)


def _fourier_kernel(u_ref, z_ref, cs_ref, pm_ref, o_ref, z_scr):
    t = pl.program_id(1)
    seq = u_ref.shape[1]

    @pl.when(t == 0)
    def _():
        xcs = _mm(u_ref[0], cs_ref[...])
        z_scr[0:seq, :] = xcs[:, :D_FOURIER].astype(BF16)
        z_scr[seq:2 * seq, :] = xcs[:, D_FOURIER:].astype(BF16)

    y = jnp.dot(pm_ref[...], z_scr[...], preferred_element_type=F32)
    o_ref[0] = (y * _silu(z_ref[0])).astype(BF16)


def _fourier_mixer(proj3, cs, pm):
    b, seq, _ = proj3.shape
    tl = min(seq, 512)
    return pl.pallas_call(
        _fourier_kernel,
        grid=(b, seq // tl),
        in_specs=[pl.BlockSpec((1, seq, D_FOURIER), lambda i, t: (i, 0, 0)),
                  pl.BlockSpec((1, tl, D_FOURIER), lambda i, t: (i, t, 1)),
                  pl.BlockSpec((D_FOURIER, 2 * D_FOURIER), lambda i, t: (0, 0)),
                  pl.BlockSpec((tl, 2 * seq), lambda i, t: (t, 0))],
        out_specs=pl.BlockSpec((1, tl, D_FOURIER), lambda i, t: (i, t, 0)),
        out_shape=jax.ShapeDtypeStruct((b, seq, D_FOURIER), BF16),
        scratch_shapes=[pltpu.VMEM((2 * seq, D_FOURIER), BF16)],
        compiler_params=pltpu.CompilerParams(dimension_semantics=("arbitrary", "arbitrary"),
                                             vmem_limit_bytes=VMEM_LIMIT),
    )(proj3, proj3, cs, pm)


def _dft_tables(seq, grid):
    def cs(n):
        k = np.arange(n)
        ang = 2.0 * np.pi * ((k[:, None] * k[None, :]) % n) / n
        return np.cos(ang), np.sin(ang)

    cc, sc = cs(FOURIER_GROUP)
    eye = np.eye(N_FOURIER_GROUPS)
    chan = np.concatenate([np.kron(eye, cc), np.kron(eye, sc)], axis=1) / math.sqrt(FOURIER_GROUP)
    if grid:
        rows = seq // GRID_W
        cr, sr = cs(rows)
        cw, sw = cs(GRID_W)
        cr, sr, cw, sw = (jnp.asarray(t, F32) for t in (cr, sr, cw, sw))
        cp = jnp.kron(cr, cw) - jnp.kron(sr, sw)
        sp = jnp.kron(sr, cw) + jnp.kron(cr, sw)
    else:
        cp, sp = (jnp.asarray(t, F32) for t in cs(seq))
    pos = jnp.concatenate([cp, -sp], axis=1) * (1.0 / math.sqrt(seq))
    return jnp.asarray(chan, F32).astype(BF16), pos.astype(BF16)


def _delta_kernel(*refs, seq, has_state_in, has_state_out):
    it = iter(refs)
    q_ref, k_ref, v_ref, zd_ref = next(it), next(it), next(it), next(it)
    cwq_ref, cwk_ref, cwv_ref = next(it), next(it), next(it)
    gi_ref, gr_ref, gtf_ref, gtb_ref, gnw_ref = next(it), next(it), next(it), next(it), next(it)
    s0_ref = next(it) if has_state_in else None
    y_ref = next(it)
    st_ref = next(it) if has_state_out else None
    qn_s, kn_s, vv_s, u_s, w_s, qg_s, kdt_s, at_s, egl_s, o_s, s_s = (next(it) for _ in range(11))

    head = pl.program_id(1)
    n_pair = seq // PAIR
    n_chunk = seq // CHUNK

    row = lax.broadcasted_iota(jnp.int32, (PAIR, HEAD_DIM), 0)

    def conv_tile(src_ref, cw_ref, r0, first, last):
        cur = src_ref[0, pl.ds(r0, PAIR), :]
        lo = pl.multiple_of(jnp.maximum(r0 - 8, 0), 8)
        hi = pl.multiple_of(jnp.minimum(r0 + PAIR, seq - 8), 8)
        before = src_ref[0, pl.ds(lo, 8), :][7:8, :]
        after = src_ref[0, pl.ds(hi, 8), :][0:1, :]
        before = jnp.where(first, 0.0, before)
        after = jnp.where(last, 0.0, after)
        prev = jnp.where(row == 0, before, pltpu.roll(cur, 1, 0))
        nxt = jnp.where(row == PAIR - 1, after, pltpu.roll(cur, PAIR - 1, 0))
        cw = cw_ref[...]
        y = prev * cw[0:1, :] + cur * cw[1:2, :] + nxt * cw[2:3, :]
        return _silu(y)

    def l2n(x):
        return x * lax.rsqrt(jnp.sum(x * x, axis=-1, keepdims=True) + EPS)

    def prep(m, carry):
        r0 = pl.multiple_of(m * PAIR, PAIR)
        first, last = m == 0, m == n_pair - 1
        rows = pl.ds(r0, PAIR)
        qn_s[rows, :] = l2n(conv_tile(q_ref, cwq_ref, r0, first, last)) * (HEAD_DIM ** -0.5)
        kn_s[rows, :] = l2n(conv_tile(k_ref, cwk_ref, r0, first, last))
        vv_s[rows, :] = conv_tile(v_ref, cwv_ref, r0, first, last)
        return carry

    lax.fori_loop(0, n_pair, prep, 0)

    ri = lax.broadcasted_iota(jnp.int32, (PAIR, PAIR), 0)
    ci = lax.broadcasted_iota(jnp.int32, (PAIR, PAIR), 1)
    same = (ri // CHUNK) == (ci // CHUNK)
    eye = (ri == ci).astype(F32)
    lane = lax.broadcasted_iota(jnp.int32, (PAIR, LANES), 1)

    def pick(tile, c):
        return jnp.sum(jnp.where(lane == c, tile, 0.0), axis=-1, keepdims=True)

    def local(m, carry):
        r0 = pl.multiple_of(m * PAIR, PAIR)
        rows = pl.ds(r0, PAIR)
        qn, kn, vv = qn_s[rows, :], kn_s[rows, :], vv_s[rows, :]
        kb16 = kn.astype(BF16)
        gram = lax.dot_general(jnp.concatenate([kb16, qn.astype(BF16)], axis=0), kb16, NT_DIMS,
                               preferred_element_type=F32)
        kk, qk = gram[:PAIR], gram[PAIR:]
        gi_t, gr_t = gi_ref[0, rows, :], gr_ref[0, rows, :]
        for d in range(N_DIR):
            c_g = head + d * N_HEADS
            g_col = pick(gi_t, c_g)
            b_col = pick(gi_t, c_g + N_DIR * N_HEADS)
            r_col = pick(gr_t, c_g)
            g_row = (gtf_ref if d == 0 else gtb_ref)[0, 0, pl.ds(m, 1), :]
            incl = same & ((ri >= ci) if d == 0 else (ri <= ci))
            decay = jnp.where(incl, jnp.exp(jnp.where(incl, g_col - g_row, 0.0)), 0.0)
            lmat = jnp.where(ri == ci, 0.0, b_col * kk * decay)
            attn = qk * decay
            tinv = eye - lmat
            power = lmat
            for _ in range(5):
                power = _mm(power, power)
                tinv = tinv + _mm(tinv, power)
            e_g = jnp.exp(g_col)
            rhs = jnp.concatenate([vv * b_col, kn * b_col * e_g], axis=1)
            uw = _mm(tinv, rhs)
            u_s[d, rows, :] = uw[:, :HEAD_DIM]
            w_s[d, rows, :] = uw[:, HEAD_DIM:].astype(BF16)
            qg_s[d, rows, :] = (qn * e_g).astype(BF16)
            kdt = jnp.transpose(kn * jnp.exp(r_col)).astype(BF16)
            kdt_s[d, 2 * m] = kdt[:, :CHUNK]
            kdt_s[d, 2 * m + 1] = kdt[:, CHUNK:]
            attn16 = attn.astype(BF16)
            at_s[d, 2 * m] = attn16[:CHUNK, :CHUNK]
            at_s[d, 2 * m + 1] = attn16[CHUNK:, CHUNK:]
            total = g_col + r_col
            egl_s[d, pl.ds(2 * m, 1), :] = jnp.broadcast_to(jnp.exp(total[0:1, :]), (1, LANES))
            egl_s[d, pl.ds(2 * m + 1, 1), :] = jnp.broadcast_to(jnp.exp(total[CHUNK:CHUNK + 1, :]), (1, LANES))
        return carry

    lax.fori_loop(0, n_pair, local, 0)

    if has_state_in:
        s_s[0] = s0_ref[0, 0, 0]
        s_s[1] = s0_ref[0, 1, 0]
    else:
        s_s[...] = jnp.zeros(s_s.shape, F32)
    o_s[...] = jnp.zeros(o_s.shape, F32)

    def step(n, carry):
        for d in range(N_DIR):
            c = n if d == 0 else n_chunk - 1 - n
            rows = pl.ds(pl.multiple_of(c * CHUNK, CHUNK), CHUNK)
            s_old = s_s[d]
            sw = jnp.dot(jnp.concatenate([w_s[d, rows, :], qg_s[d, rows, :]], axis=0), s_old.astype(BF16),
                         preferred_element_type=F32)
            v_new = u_s[d, rows, :] - sw[:CHUNK]
            vn16 = v_new.astype(BF16)
            o_c = sw[CHUNK:] + jnp.dot(at_s[d, c], vn16, preferred_element_type=F32)
            o_s[rows, :] = o_s[rows, :] + o_c
            s_s[d] = s_old * egl_s[d, pl.ds(c, 1), :] + jnp.dot(kdt_s[d, c], vn16, preferred_element_type=F32)
        return carry

    lax.fori_loop(0, n_chunk, step, 0)

    if has_state_out:
        st_ref[0, 0, 0] = s_s[0]
        st_ref[0, 1, 0] = s_s[1]

    def finish(m, carry):
        rows = pl.ds(pl.multiple_of(m * PAIR, PAIR), PAIR)
        o = o_s[rows, :]
        y = o * lax.rsqrt(jnp.mean(o * o, axis=-1, keepdims=True) + EPS) * gnw_ref[...]
        y_ref[0, rows, :] = (y * _silu(zd_ref[0, rows, :])).astype(BF16)
        return carry

    lax.fori_loop(0, n_pair, finish, 0)


def _delta_mixer(proj3, conv_w, gi3, gr3, git3, gnorm_w, s0, want_state):
    b, seq, _ = proj3.shape
    n_pair, n_chunk = seq // PAIR, seq // CHUNK
    tok = lambda col0: pl.BlockSpec((1, seq, HEAD_DIM), lambda i, h: (i, 0, col0 + h))
    cw = lambda col0: pl.BlockSpec((CONV_K, HEAD_DIM), lambda i, h: (0, col0 + h))
    in_specs = [tok(COL_Q), tok(COL_K), tok(COL_V), tok(COL_ZD),
                cw(0), cw(N_HEADS), cw(2 * N_HEADS),
                pl.BlockSpec((1, seq, LANES), lambda i, h: (i, 0, 0)),
                pl.BlockSpec((1, seq, LANES), lambda i, h: (i, 0, 0)),
                pl.BlockSpec((1, 1, n_pair, LANES), lambda i, h: (h, i, 0, 0)),
                pl.BlockSpec((1, 1, n_pair, LANES), lambda i, h: (N_HEADS + h, i, 0, 0)),
                pl.BlockSpec((1, HEAD_DIM), lambda i, h: (0, 0))]
    args = [proj3, proj3, proj3, proj3, conv_w, conv_w, conv_w, gi3, gr3, git3, git3, gnorm_w]
    state_spec = pl.BlockSpec((1, N_DIR, 1, HEAD_DIM, HEAD_DIM), lambda i, h: (i, 0, h, 0, 0))
    if s0 is not None:
        in_specs.append(state_spec)
        args.append(s0)
    out_specs = [pl.BlockSpec((1, seq, HEAD_DIM), lambda i, h: (i, 0, h))]
    out_shape = [jax.ShapeDtypeStruct((b, seq, D_DELTA), BF16)]
    if want_state:
        out_specs.append(state_spec)
        out_shape.append(jax.ShapeDtypeStruct((b, N_DIR, N_HEADS, HEAD_DIM, HEAD_DIM), F32))
    scratch = [pltpu.VMEM((seq, HEAD_DIM), F32),
               pltpu.VMEM((seq, HEAD_DIM), F32),
               pltpu.VMEM((seq, HEAD_DIM), F32),
               pltpu.VMEM((N_DIR, seq, HEAD_DIM), F32),
               pltpu.VMEM((N_DIR, seq, HEAD_DIM), BF16),
               pltpu.VMEM((N_DIR, seq, HEAD_DIM), BF16),
               pltpu.VMEM((N_DIR, n_chunk, HEAD_DIM, CHUNK), BF16),
               pltpu.VMEM((N_DIR, n_chunk, CHUNK, CHUNK), BF16),
               pltpu.VMEM((N_DIR, n_chunk, LANES), F32),
               pltpu.VMEM((seq, HEAD_DIM), F32),
               pltpu.VMEM((N_DIR, HEAD_DIM, HEAD_DIM), F32)]
    res = pl.pallas_call(
        functools.partial(_delta_kernel, seq=seq, has_state_in=s0 is not None, has_state_out=want_state),
        grid=(b, N_HEADS),
        in_specs=in_specs,
        out_specs=out_specs,
        out_shape=out_shape,
        scratch_shapes=scratch,
        compiler_params=pltpu.CompilerParams(dimension_semantics=("arbitrary", "arbitrary"),
                                             vmem_limit_bytes=VMEM_LIMIT),
    )(*args)
    return (res[0], res[1]) if want_state else (res[0], None)


def _outproj_kernel(yf_ref, yd_ref, wf_ref, wd_ref, x_ref, gate_ref, fnw_ref, o_ref, *, final):
    y = jnp.dot(yf_ref[...], wf_ref[...], preferred_element_type=F32)
    y = y + jnp.dot(yd_ref[...], wd_ref[...], preferred_element_type=F32)
    x_new = x_ref[...] + gate_ref[0] * y
    if final:
        ms = jnp.mean(x_new * x_new, axis=-1, keepdims=True)
        x_new = x_new * lax.rsqrt(ms + EPS) * fnw_ref[...]
    o_ref[...] = x_new


def _out_projection(yf2, yd2, wf, wd, x2, mods, mod_row, final_norm_w, seq_len, final):
    m = x2.shape[0]
    tm = 256
    assert m % tm == 0 and seq_len % tm == 0
    if mod_row is None:
        row = lambda i: (i * tm) // seq_len
    else:
        row = lambda i: mod_row
    return pl.pallas_call(
        functools.partial(_outproj_kernel, final=final),
        grid=(m // tm,),
        in_specs=[pl.BlockSpec((tm, D_FOURIER), lambda i: (i, 0)),
                  pl.BlockSpec((tm, D_DELTA), lambda i: (i, 0)),
                  pl.BlockSpec((D_FOURIER, D_MODEL), lambda i: (0, 0)),
                  pl.BlockSpec((D_DELTA, D_MODEL), lambda i: (0, 0)),
                  pl.BlockSpec((tm, D_MODEL), lambda i: (i, 0)),
                  pl.BlockSpec((1, 1, D_MODEL), lambda i: (row(i), 0, 2)),
                  pl.BlockSpec((1, D_MODEL), lambda i: (0, 0))],
        out_specs=pl.BlockSpec((tm, D_MODEL), lambda i: (i, 0)),
        out_shape=jax.ShapeDtypeStruct((m, D_MODEL), F32),
        compiler_params=pltpu.CompilerParams(dimension_semantics=("arbitrary",),
                                             vmem_limit_bytes=VMEM_LIMIT),
    )(yf2, yd2, wf, wd, x2, mods, final_norm_w)


def _mixer_layer(x2, batch, seq, mods, mod_row, norm_w, w_bf, wab, wabt, gate_params, conv_w,
                 gnorm_w, wf, wd, dft, s0, want_state, final_norm_w, final):
    alog_r, dt_r, alog_c, dt_c = gate_params
    proj, gi, gr, git = _in_projection(x2, mods, mod_row, norm_w, w_bf, wab, wabt,
                                       alog_r, dt_r, alog_c, dt_c, seq)
    proj3 = proj.reshape(batch, seq, D_MAIN)
    yf = _fourier_mixer(proj3, *dft)
    yd, state = _delta_mixer(proj3, conv_w, gi.reshape(batch, seq, LANES), gr.reshape(batch, seq, LANES),
                             git.reshape(LANES, batch, seq // PAIR, LANES), gnorm_w, s0, want_state)
    x_new = _out_projection(yf.reshape(batch * seq, D_FOURIER), yd.reshape(batch * seq, D_DELTA),
                            wf, wd, x2, mods, mod_row, final_norm_w, seq, final)
    return x_new, state


def kernel(x_prompt, x_sample, state_ctx, c, c_ctx, norm_w, w_mod, b_mod, w_in, conv_w, a_log, dt_bias,
           gnorm_w, w_out, final_norm_w):
    depth = w_in.shape[0]
    bp, lp, _ = x_prompt.shape
    bs, ls, _ = x_sample.shape
    ctx_row = bs

    cvec = jnp.zeros((8, D_MODEL), F32).at[:bs].set(c).at[ctx_row].set(c_ctx)
    mods = _modulation(cvec, w_mod, b_mod)
    dft_ctx = _dft_tables(lp, False)
    dft_lat = _dft_tables(ls, True)
    fnw = final_norm_w.reshape(1, D_MODEL)

    xp = x_prompt.reshape(bp * lp, D_MODEL)
    xs = x_sample.reshape(bs * ls, D_MODEL)
    states = []
    for i in range(depth):
        w_bf = w_in[i].astype(BF16)
        w_gate = jnp.pad(w_in[i][:, D_MAIN:], ((0, 0), (0, LANES - N_GATE)))
        wab = w_gate.astype(BF16)
        wabt = w_gate.T.astype(BF16)
        pad = (0, LANES - N_DIR * N_HEADS)
        alog = jnp.pad(a_log[i].reshape(-1), pad)
        dtb = jnp.pad(dt_bias[i].reshape(-1), pad)
        gate_params = (alog.reshape(1, LANES), dtb.reshape(1, LANES),
                       alog.reshape(LANES, 1), dtb.reshape(LANES, 1))
        wf = w_out[i][:D_FOURIER].astype(BF16)
        wd = w_out[i][D_FOURIER:].astype(BF16)
        mods_i = mods[i].reshape(8, 1, 3 * D_MODEL)
        nw = norm_w[i].reshape(1, D_MODEL)
        gnw = gnorm_w[i].reshape(1, HEAD_DIM)
        final = i == depth - 1
        common = (nw, w_bf, wab, wabt, gate_params, conv_w[i], gnw, wf, wd)
        xp, s_new = _mixer_layer(xp, bp, lp, mods_i, ctx_row, *common, dft_ctx, None, True, fnw, final)
        states.append(s_new)
        xs, _ = _mixer_layer(xs, bs, ls, mods_i, None, *common, dft_lat, state_ctx[:, i], False, fnw, final)
    y_prompt = xp.reshape(bp, lp, D_MODEL)
    y_sample = xs.reshape(bs, ls, D_MODEL)
    state_new = jnp.stack(states, axis=1).astype(x_prompt.dtype)
    return (y_prompt, y_sample, state_new)
```

```python
import functools
import math

import numpy as np
import jax
import jax.numpy as jnp
from jax import lax
from jax.experimental import pallas as pl
from jax.experimental.pallas import tpu as pltpu

D_MODEL = 2048
GRID_W = 64
D_FOURIER = 512
N_FOURIER_GROUPS = 4
FOURIER_GROUP = 128
D_DELTA = 1536
HEAD_DIM = 128
N_HEADS = 12
N_DIR = 2
CONV_K = 3
CHUNK = 64
EPS = 1e-6

PAIR = 2 * CHUNK
D_MAIN = 2 * D_FOURIER + 4 * D_DELTA
N_GATE = 2 * N_DIR * N_HEADS
LANES = 128
COL_Q = (2 * D_FOURIER) // LANES
COL_K = COL_Q + N_HEADS
COL_V = COL_K + N_HEADS
COL_ZD = COL_V + N_HEADS
VMEM_LIMIT = 56 * 1024 * 1024

BF16 = jnp.bfloat16
F32 = jnp.float32
NT_DIMS = (((1,), (1,)), ((), ()))


def _mm(a, b):
    return jnp.dot(a.astype(BF16), b.astype(BF16), preferred_element_type=F32)


def _silu(x):
    return x / (1.0 + jnp.exp(-x))


def _mod_kernel(c_ref, w_ref, b_ref, o_ref):
    s = _silu(c_ref[...])
    o_ref[0] = _mm(s, w_ref[0]) + b_ref[0]


def _modulation(cvec, w_mod, b_mod):
    depth, _, n = w_mod.shape
    tn = 512
    return pl.pallas_call(
        _mod_kernel,
        grid=(depth, n // tn),
        in_specs=[pl.BlockSpec((8, D_MODEL), lambda l, j: (0, 0)),
                  pl.BlockSpec((1, D_MODEL, tn), lambda l, j: (l, 0, j)),
                  pl.BlockSpec((1, 1, tn), lambda l, j: (l, 0, j))],
        out_specs=pl.BlockSpec((1, 8, tn), lambda l, j: (l, 0, j)),
        out_shape=jax.ShapeDtypeStruct((depth, 8, n), F32),
        compiler_params=pltpu.CompilerParams(dimension_semantics=("arbitrary", "arbitrary")),
    )(cvec, w_mod, b_mod.reshape(depth, 1, n))


def _chunk_scans(lg, axis):
    n = lg.shape[axis]
    pos = lax.broadcasted_iota(jnp.int32, lg.shape, axis) % CHUNK
    pre, suf = lg, lg
    s = 1
    while s < CHUNK:
        pre = pre + jnp.where(pos >= s, pltpu.roll(pre, s, axis), 0.0)
        suf = suf + jnp.where(pos < CHUNK - s, pltpu.roll(suf, n - s, axis), 0.0)
        s *= 2
    return pre, suf


def _gate_tables(ab, a_log, dt_bias, gate_axis):
    tok_axis = 1 - gate_axis
    col = lax.broadcasted_iota(jnp.int32, ab.shape, gate_axis)
    xa = ab + dt_bias
    softplus = jnp.maximum(xa, 0.0) + jnp.log1p(jnp.exp(-jnp.abs(xa)))
    lg = jnp.where(col < N_DIR * N_HEADS, -jnp.exp(a_log) * softplus, 0.0)
    beta = 1.0 / (1.0 + jnp.exp(-ab))
    pre, suf = _chunk_scans(lg, tok_axis)
    fwd = col < N_HEADS
    dec = col < N_DIR * N_HEADS
    incl = jnp.where(fwd, pre, jnp.where(dec, suf, jnp.where(col < N_GATE, beta, 0.0)))
    rest = jnp.where(fwd, suf - lg, jnp.where(dec, pre - lg, 0.0))
    return incl, rest


def _inproj_kernel(x_ref, shift_ref, scale_ref, nw_ref, w_ref, wab_ref, wabt_ref,
                   alog_r_ref, dt_r_ref, alog_c_ref, dt_c_ref,
                   proj_ref, gi_ref, gr_ref, git_ref, h_scr, *, slab):
    j = pl.program_id(1)

    @pl.when(j == 0)
    def _():
        tm = x_ref.shape[0]
        for s in range(tm // slab):
            rows = pl.ds(s * slab, slab)
            xf = x_ref[rows, :]
            ms = jnp.mean(xf * xf, axis=-1, keepdims=True)
            y = xf * lax.rsqrt(ms + EPS) * nw_ref[...]
            hb = (y * (1.0 + scale_ref[0]) + shift_ref[0]).astype(BF16)
            h_scr[rows, :] = hb
            ab = jnp.dot(hb, wab_ref[...], preferred_element_type=F32)
            gi, gr = _gate_tables(ab, alog_r_ref[...], dt_r_ref[...], 1)
            gi_ref[rows, :] = gi
            gr_ref[rows, :] = gr
            abt = lax.dot_general(wabt_ref[...], hb, NT_DIMS, preferred_element_type=F32)
            git, _ = _gate_tables(abt, alog_c_ref[...], dt_c_ref[...], 0)
            git_ref[:, s * slab:(s + 1) * slab] = git

    proj_ref[...] = jnp.dot(h_scr[...], w_ref[...], preferred_element_type=F32)


def _in_projection(x2, mods, mod_row, norm_w, w_bf, wab, wabt, alog_r, dt_r, alog_c, dt_c, seq_len):
    m = x2.shape[0]
    tm, tn, slab = 1024, 512, 256
    assert m % tm == 0 and D_MAIN % tn == 0 and (seq_len % tm == 0 or tm % seq_len == 0)
    if mod_row is None:
        row = lambda i: (i * tm) // seq_len
    else:
        row = lambda i: mod_row
    const = lambda i, j: (0, 0)
    return pl.pallas_call(
        functools.partial(_inproj_kernel, slab=slab),
        grid=(m // tm, D_MAIN // tn),
        in_specs=[pl.BlockSpec((tm, D_MODEL), lambda i, j: (i, 0)),
                  pl.BlockSpec((1, 1, D_MODEL), lambda i, j: (row(i), 0, 0)),
                  pl.BlockSpec((1, 1, D_MODEL), lambda i, j: (row(i), 0, 1)),
                  pl.BlockSpec((1, D_MODEL), const),
                  pl.BlockSpec((D_MODEL, tn), lambda i, j: (0, j)),
                  pl.BlockSpec((D_MODEL, LANES), const),
                  pl.BlockSpec((LANES, D_MODEL), const),
                  pl.BlockSpec((1, LANES), const),
                  pl.BlockSpec((1, LANES), const),
                  pl.BlockSpec((LANES, 1), const),
                  pl.BlockSpec((LANES, 1), const)],
        out_specs=[pl.BlockSpec((tm, tn), lambda i, j: (i, j)),
                   pl.BlockSpec((tm, LANES), lambda i, j: (i, 0)),
                   pl.BlockSpec((tm, LANES), lambda i, j: (i, 0)),
                   pl.BlockSpec((LANES, tm), lambda i, j: (0, i))],
        out_shape=[jax.ShapeDtypeStruct((m, D_MAIN), F32),
                   jax.ShapeDtypeStruct((m, LANES), F32),
                   jax.ShapeDtypeStruct((m, LANES), F32),
                   jax.ShapeDtypeStruct((LANES, m), F32)],
        scratch_shapes=[pltpu.VMEM((tm, D_MODEL), BF16)],
        compiler_params=pltpu.CompilerParams(dimension_semantics=("arbitrary", "arbitrary"),
                                             vmem_limit_bytes=VMEM_LIMIT),
    )(x2, mods, mods, norm_w, w_bf, wab, wabt, alog_r, dt_r, alog_c, dt_c)


def _fourier_kernel(u_ref, z_ref, cs_ref, pm_ref, o_ref, z_scr):
    t = pl.program_id(1)
    seq = u_ref.shape[1]

    @pl.when(t == 0)
    def _():
        xcs = _mm(u_ref[0], cs_ref[...])
        z_scr[0:seq, :] = xcs[:, :D_FOURIER].astype(BF16)
        z_scr[seq:2 * seq, :] = xcs[:, D_FOURIER:].astype(BF16)

    y = jnp.dot(pm_ref[...], z_scr[...], preferred_element_type=F32)
    o_ref[0] = (y * _silu(z_ref[0])).astype(BF16)


def _fourier_mixer(proj3, cs, pm):
    b, seq, _ = proj3.shape
    tl = min(seq, 512)
    return pl.pallas_call(
        _fourier_kernel,
        grid=(b, seq // tl),
        in_specs=[pl.BlockSpec((1, seq, D_FOURIER), lambda i, t: (i, 0, 0)),
                  pl.BlockSpec((1, tl, D_FOURIER), lambda i, t: (i, t, 1)),
                  pl.BlockSpec((D_FOURIER, 2 * D_FOURIER), lambda i, t: (0, 0)),
                  pl.BlockSpec((tl, 2 * seq), lambda i, t: (t, 0))],
        out_specs=pl.BlockSpec((1, tl, D_FOURIER), lambda i, t: (i, t, 0)),
        out_shape=jax.ShapeDtypeStruct((b, seq, D_FOURIER), BF16),
        scratch_shapes=[pltpu.VMEM((2 * seq, D_FOURIER), BF16)],
        compiler_params=pltpu.CompilerParams(dimension_semantics=("arbitrary", "arbitrary"),
                                             vmem_limit_bytes=VMEM_LIMIT),
    )(proj3, proj3, cs, pm)


def _dft_tables(seq, grid):
    def cs(n):
        k = np.arange(n)
        ang = 2.0 * np.pi * ((k[:, None] * k[None, :]) % n) / n
        return np.cos(ang), np.sin(ang)

    cc, sc = cs(FOURIER_GROUP)
    eye = np.eye(N_FOURIER_GROUPS)
    chan = np.concatenate([np.kron(eye, cc), np.kron(eye, sc)], axis=1) / math.sqrt(FOURIER_GROUP)
    if grid:
        rows = seq // GRID_W
        cr, sr = cs(rows)
        cw, sw = cs(GRID_W)
        cr, sr, cw, sw = (jnp.asarray(t, F32) for t in (cr, sr, cw, sw))
        cp = jnp.kron(cr, cw) - jnp.kron(sr, sw)
        sp = jnp.kron(sr, cw) + jnp.kron(cr, sw)
    else:
        cp, sp = (jnp.asarray(t, F32) for t in cs(seq))
    pos = jnp.concatenate([cp, -sp], axis=1) * (1.0 / math.sqrt(seq))
    return jnp.asarray(chan, F32).astype(BF16), pos.astype(BF16)


def _delta_kernel(*refs, seq, heads, unroll, has_state_in, has_state_out):
    it = iter(refs)
    take = lambda n: [next(it) for _ in range(n)]
    q_refs, k_refs, v_refs, zd_refs = take(heads), take(heads), take(heads), take(heads)
    cwq_refs, cwk_refs, cwv_refs = take(heads), take(heads), take(heads)
    gi_ref, gr_ref = next(it), next(it)
    gt_refs = [take(N_DIR) for _ in range(heads)]
    gnw_ref = next(it)
    s0_ref = next(it) if has_state_in else None
    y_ref = next(it)
    st_ref = next(it) if has_state_out else None
    qn_s, kn_s, vv_s, u_s, w_s, qg_s, kdt_s, at_s, egl_s, o_s, s_s = take(11)

    head0 = pl.program_id(1) * heads
    n_pair = seq // PAIR
    n_chunk = seq // CHUNK

    row = lax.broadcasted_iota(jnp.int32, (PAIR, HEAD_DIM), 0)

    def conv_tile(src_ref, cw_ref, r0, first, last):
        cur = src_ref[0, pl.ds(r0, PAIR), :]
        lo = pl.multiple_of(jnp.maximum(r0 - 8, 0), 8)
        hi = pl.multiple_of(jnp.minimum(r0 + PAIR, seq - 8), 8)
        before = src_ref[0, pl.ds(lo, 8), :][7:8, :]
        after = src_ref[0, pl.ds(hi, 8), :][0:1, :]
        before = jnp.where(first, 0.0, before)
        after = jnp.where(last, 0.0, after)
        prev = jnp.where(row == 0, before, pltpu.roll(cur, 1, 0))
        nxt = jnp.where(row == PAIR - 1, after, pltpu.roll(cur, PAIR - 1, 0))
        cw = cw_ref[...]
        y = prev * cw[0:1, :] + cur * cw[1:2, :] + nxt * cw[2:3, :]
        return _silu(y)

    def l2n(x):
        return x * lax.rsqrt(jnp.sum(x * x, axis=-1, keepdims=True) + EPS)

    def prep(m, carry):
        r0 = pl.multiple_of(m * PAIR, PAIR)
        first, last = m == 0, m == n_pair - 1
        rows = pl.ds(r0, PAIR)
        for g in range(heads):
            qn_s[g, rows, :] = l2n(conv_tile(q_refs[g], cwq_refs[g], r0, first, last)) * (HEAD_DIM ** -0.5)
            kn_s[g, rows, :] = l2n(conv_tile(k_refs[g], cwk_refs[g], r0, first, last))
            vv_s[g, rows, :] = conv_tile(v_refs[g], cwv_refs[g], r0, first, last)
        return carry

    lax.fori_loop(0, n_pair, prep, 0)

    ri = lax.broadcasted_iota(jnp.int32, (PAIR, PAIR), 0)
    ci = lax.broadcasted_iota(jnp.int32, (PAIR, PAIR), 1)
    same = (ri // CHUNK) == (ci // CHUNK)
    eye = (ri == ci).astype(F32)
    lane = lax.broadcasted_iota(jnp.int32, (PAIR, LANES), 1)

    def pick(tile, c):
        return jnp.sum(jnp.where(lane == c, tile, 0.0), axis=-1, keepdims=True)

    def local(t, carry):
        tiles = [(g, t * unroll + k) for g in range(heads) for k in range(unroll)]
        base = []
        for g, m in tiles:
            rows = pl.ds(pl.multiple_of(m * PAIR, PAIR), PAIR)
            qn, kn, vv = qn_s[g, rows, :], kn_s[g, rows, :], vv_s[g, rows, :]
            kb16 = kn.astype(BF16)
            gram = lax.dot_general(jnp.concatenate([kb16, qn.astype(BF16)], axis=0), kb16, NT_DIMS,
                                   preferred_element_type=F32)
            base.append((g, m, rows, qn, kn, vv, gram[:PAIR], gram[PAIR:],
                         gi_ref[0, rows, :], gr_ref[0, rows, :]))
        chains = []
        for g, m, rows, qn, kn, vv, kk, qk, gi_t, gr_t in base:
            for d in range(N_DIR):
                c_g = head0 + g + d * N_HEADS
                g_col = pick(gi_t, c_g)
                b_col = pick(gi_t, c_g + N_DIR * N_HEADS)
                r_col = pick(gr_t, c_g)
                g_row = gt_refs[g][d][0, 0, pl.ds(m, 1), :]
                incl = same & ((ri >= ci) if d == 0 else (ri <= ci))
                decay = jnp.where(incl, jnp.exp(jnp.where(incl, g_col - g_row, 0.0)), 0.0)
                lmat = jnp.where(ri == ci, 0.0, b_col * kk * decay)
                attn16 = (qk * decay).astype(BF16)
                at_s[g, d, 2 * m] = attn16[:CHUNK, :CHUNK]
                at_s[g, d, 2 * m + 1] = attn16[CHUNK:, CHUNK:]
                e_g = jnp.exp(g_col)
                qg_s[g, d, rows, :] = (qn * e_g).astype(BF16)
                kdt = jnp.transpose(kn * jnp.exp(r_col)).astype(BF16)
                kdt_s[g, d, 2 * m] = kdt[:, :CHUNK]
                kdt_s[g, d, 2 * m + 1] = kdt[:, CHUNK:]
                total = g_col + r_col
                egl_s[g, d, pl.ds(2 * m, 1), :] = jnp.broadcast_to(jnp.exp(total[0:1, :]), (1, LANES))
                egl_s[g, d, pl.ds(2 * m + 1, 1), :] = jnp.broadcast_to(jnp.exp(total[CHUNK:CHUNK + 1, :]),
                                                                       (1, LANES))
                rhs = jnp.concatenate([vv * b_col, kn * b_col * e_g], axis=1).astype(BF16)
                chains.append((g, d, rows, lmat, rhs))
        tinvs = [eye - lmat for _, _, _, lmat, _ in chains]
        powers = [lmat for _, _, _, lmat, _ in chains]
        for _ in range(5):
            powers = [_mm(p, p) for p in powers]
            tinvs = [ti + _mm(ti, p) for ti, p in zip(tinvs, powers)]
        uws = [jnp.dot(ti.astype(BF16), rhs, preferred_element_type=F32)
               for ti, (_, _, _, _, rhs) in zip(tinvs, chains)]
        for uw, (g, d, rows, _, _) in zip(uws, chains):
            u_s[g, d, rows, :] = uw[:, :HEAD_DIM]
            w_s[g, d, rows, :] = uw[:, HEAD_DIM:].astype(BF16)
        return carry

    lax.fori_loop(0, n_pair // unroll, local, 0)

    for g in range(heads):
        for d in range(N_DIR):
            s_s[g, d] = s0_ref[0, d, g] if has_state_in else jnp.zeros((HEAD_DIM, HEAD_DIM), F32)

    def step(n, carry):
        chains = [(g, d) for g in range(heads) for d in range(N_DIR)]
        where = []
        for g, d in chains:
            c = n if d == 0 else n_chunk - 1 - n
            where.append((c, pl.ds(pl.multiple_of(c * CHUNK, CHUNK), CHUNK)))
        s_old = [s_s[g, d] for g, d in chains]
        sws = [jnp.dot(jnp.concatenate([w_s[g, d, rows, :], qg_s[g, d, rows, :]], axis=0),
                       so.astype(BF16), preferred_element_type=F32)
               for (g, d), (c, rows), so in zip(chains, where, s_old)]
        vns = [(u_s[g, d, rows, :] - sw[:CHUNK]).astype(BF16)
               for (g, d), (c, rows), sw in zip(chains, where, sws)]
        ocs = [sw[CHUNK:] + jnp.dot(at_s[g, d, c], vn, preferred_element_type=F32)
               for (g, d), (c, rows), sw, vn in zip(chains, where, sws, vns)]
        sns = [so * egl_s[g, d, pl.ds(c, 1), :] + jnp.dot(kdt_s[g, d, c], vn, preferred_element_type=F32)
               for (g, d), (c, rows), so, vn in zip(chains, where, s_old, vns)]
        for (g, d), (c, rows), oc, sn in zip(chains, where, ocs, sns):
            o_s[g, d, rows, :] = oc
            s_s[g, d] = sn
        return carry

    lax.fori_loop(0, n_chunk, step, 0)

    if has_state_out:
        for g in range(heads):
            for d in range(N_DIR):
                st_ref[0, d, g] = s_s[g, d]

    def finish(m, carry):
        rows = pl.ds(pl.multiple_of(m * PAIR, PAIR), PAIR)
        for g in range(heads):
            o = o_s[g, 0, rows, :] + o_s[g, 1, rows, :]
            y = o * lax.rsqrt(jnp.mean(o * o, axis=-1, keepdims=True) + EPS) * gnw_ref[...]
            y_ref[0, rows, g * HEAD_DIM:(g + 1) * HEAD_DIM] = (y * _silu(zd_refs[g][0, rows, :])).astype(BF16)
        return carry

    lax.fori_loop(0, n_pair, finish, 0)


def _delta_mixer(proj3, conv_w, gi3, gr3, git4, gnorm_w, s0, want_state, heads, unroll):
    b, seq, _ = proj3.shape
    n_pair, n_chunk = seq // PAIR, seq // CHUNK
    assert N_HEADS % heads == 0 and n_pair % unroll == 0
    slots = range(heads)
    tok = lambda col0: [pl.BlockSpec((1, seq, HEAD_DIM), lambda i, h, g=g: (i, 0, col0 + h * heads + g))
                        for g in slots]
    cw = lambda col0: [pl.BlockSpec((CONV_K, HEAD_DIM), lambda i, h, g=g: (0, col0 + h * heads + g))
                       for g in slots]
    gate_rows = [pl.BlockSpec((1, 1, n_pair, LANES), lambda i, h, g=g, d=d: (d * N_HEADS + h * heads + g, i, 0, 0))
                 for g in slots for d in range(N_DIR)]
    in_specs = (tok(COL_Q) + tok(COL_K) + tok(COL_V) + tok(COL_ZD) + cw(0) + cw(N_HEADS) + cw(2 * N_HEADS)
                + [pl.BlockSpec((1, seq, LANES), lambda i, h: (i, 0, 0)),
                   pl.BlockSpec((1, seq, LANES), lambda i, h: (i, 0, 0))]
                + gate_rows + [pl.BlockSpec((1, HEAD_DIM), lambda i, h: (0, 0))])
    args = [proj3] * (4 * heads) + [conv_w] * (3 * heads) + [gi3, gr3] + [git4] * (N_DIR * heads) + [gnorm_w]
    state_spec = pl.BlockSpec((1, N_DIR, heads, HEAD_DIM, HEAD_DIM), lambda i, h: (i, 0, h, 0, 0))
    if s0 is not None:
        in_specs.append(state_spec)
        args.append(s0)
    out_specs = [pl.BlockSpec((1, seq, heads * HEAD_DIM), lambda i, h: (i, 0, h))]
    out_shape = [jax.ShapeDtypeStruct((b, seq, D_DELTA), BF16)]
    if want_state:
        out_specs.append(state_spec)
        out_shape.append(jax.ShapeDtypeStruct((b, N_DIR, N_HEADS, HEAD_DIM, HEAD_DIM), F32))
    scratch = [pltpu.VMEM((heads, seq, HEAD_DIM), F32),
               pltpu.VMEM((heads, seq, HEAD_DIM), F32),
               pltpu.VMEM((heads, seq, HEAD_DIM), F32),
               pltpu.VMEM((heads, N_DIR, seq, HEAD_DIM), F32),
               pltpu.VMEM((heads, N_DIR, seq, HEAD_DIM), BF16),
               pltpu.VMEM((heads, N_DIR, seq, HEAD_DIM), BF16),
               pltpu.VMEM((heads, N_DIR, n_chunk, HEAD_DIM, CHUNK), BF16),
               pltpu.VMEM((heads, N_DIR, n_chunk, CHUNK, CHUNK), BF16),
               pltpu.VMEM((heads, N_DIR, n_chunk, LANES), F32),
               pltpu.VMEM((heads, N_DIR, seq, HEAD_DIM), F32),
               pltpu.VMEM((heads, N_DIR, HEAD_DIM, HEAD_DIM), F32)]
    res = pl.pallas_call(
        functools.partial(_delta_kernel, seq=seq, heads=heads, unroll=unroll,
                          has_state_in=s0 is not None, has_state_out=want_state),
        grid=(b, N_HEADS // heads),
        in_specs=in_specs,
        out_specs=out_specs,
        out_shape=out_shape,
        scratch_shapes=scratch,
        compiler_params=pltpu.CompilerParams(dimension_semantics=("arbitrary", "arbitrary"),
                                             vmem_limit_bytes=VMEM_LIMIT),
    )(*args)
    return (res[0], res[1]) if want_state else (res[0], None)


def _outproj_kernel(yf_ref, yd_ref, wf_ref, wd_ref, x_ref, gate_ref, fnw_ref, o_ref, *, final):
    y = jnp.dot(yf_ref[...], wf_ref[...], preferred_element_type=F32)
    y = y + jnp.dot(yd_ref[...], wd_ref[...], preferred_element_type=F32)
    x_new = x_ref[...] + gate_ref[0] * y
    if final:
        ms = jnp.mean(x_new * x_new, axis=-1, keepdims=True)
        x_new = x_new * lax.rsqrt(ms + EPS) * fnw_ref[...]
    o_ref[...] = x_new


def _out_projection(yf2, yd2, wf, wd, x2, mods, mod_row, final_norm_w, seq_len, final):
    m = x2.shape[0]
    tm = 256
    assert m % tm == 0 and seq_len % tm == 0
    if mod_row is None:
        row = lambda i: (i * tm) // seq_len
    else:
        row = lambda i: mod_row
    return pl.pallas_call(
        functools.partial(_outproj_kernel, final=final),
        grid=(m // tm,),
        in_specs=[pl.BlockSpec((tm, D_FOURIER), lambda i: (i, 0)),
                  pl.BlockSpec((tm, D_DELTA), lambda i: (i, 0)),
                  pl.BlockSpec((D_FOURIER, D_MODEL), lambda i: (0, 0)),
                  pl.BlockSpec((D_DELTA, D_MODEL), lambda i: (0, 0)),
                  pl.BlockSpec((tm, D_MODEL), lambda i: (i, 0)),
                  pl.BlockSpec((1, 1, D_MODEL), lambda i: (row(i), 0, 2)),
                  pl.BlockSpec((1, D_MODEL), lambda i: (0, 0))],
        out_specs=pl.BlockSpec((tm, D_MODEL), lambda i: (i, 0)),
        out_shape=jax.ShapeDtypeStruct((m, D_MODEL), F32),
        compiler_params=pltpu.CompilerParams(dimension_semantics=("arbitrary",),
                                             vmem_limit_bytes=VMEM_LIMIT),
    )(yf2, yd2, wf, wd, x2, mods, final_norm_w)


def _mixer_layer(x2, batch, seq, mods, mod_row, norm_w, w_bf, wab, wabt, gate_params, conv_w,
                 gnorm_w, wf, wd, dft, s0, want_state, final_norm_w, final):
    alog_r, dt_r, alog_c, dt_c = gate_params
    proj, gi, gr, git = _in_projection(x2, mods, mod_row, norm_w, w_bf, wab, wabt,
                                       alog_r, dt_r, alog_c, dt_c, seq)
    proj3 = proj.reshape(batch, seq, D_MAIN)
    yf = _fourier_mixer(proj3, *dft)
    yd, state = _delta_mixer(proj3, conv_w, gi.reshape(batch, seq, LANES), gr.reshape(batch, seq, LANES),
                             git.reshape(LANES, batch, seq // PAIR, LANES), gnorm_w, s0, want_state,
                             heads=2 if seq > 4 * PAIR else 4, unroll=2)
    x_new = _out_projection(yf.reshape(batch * seq, D_FOURIER), yd.reshape(batch * seq, D_DELTA),
                            wf, wd, x2, mods, mod_row, final_norm_w, seq, final)
    return x_new, state


def kernel(x_prompt, x_sample, state_ctx, c, c_ctx, norm_w, w_mod, b_mod, w_in, conv_w, a_log, dt_bias,
           gnorm_w, w_out, final_norm_w):
    depth = w_in.shape[0]
    bp, lp, _ = x_prompt.shape
    bs, ls, _ = x_sample.shape
    ctx_row = bs

    cvec = jnp.zeros((8, D_MODEL), F32).at[:bs].set(c).at[ctx_row].set(c_ctx)
    mods = _modulation(cvec, w_mod, b_mod)
    dft_ctx = _dft_tables(lp, False)
    dft_lat = _dft_tables(ls, True)
    fnw = final_norm_w.reshape(1, D_MODEL)

    xp = x_prompt.reshape(bp * lp, D_MODEL)
    xs = x_sample.reshape(bs * ls, D_MODEL)
    states = []
    for i in range(depth):
        w_bf = w_in[i].astype(BF16)
        w_gate = jnp.pad(w_in[i][:, D_MAIN:], ((0, 0), (0, LANES - N_GATE)))
        wab = w_gate.astype(BF16)
        wabt = w_gate.T.astype(BF16)
        pad = (0, LANES - N_DIR * N_HEADS)
        alog = jnp.pad(a_log[i].reshape(-1), pad)
        dtb = jnp.pad(dt_bias[i].reshape(-1), pad)
        gate_params = (alog.reshape(1, LANES), dtb.reshape(1, LANES),
                       alog.reshape(LANES, 1), dtb.reshape(LANES, 1))
        wf = w_out[i][:D_FOURIER].astype(BF16)
        wd = w_out[i][D_FOURIER:].astype(BF16)
        mods_i = mods[i].reshape(8, 1, 3 * D_MODEL)
        nw = norm_w[i].reshape(1, D_MODEL)
        gnw = gnorm_w[i].reshape(1, HEAD_DIM)
        final = i == depth - 1
        common = (nw, w_bf, wab, wabt, gate_params, conv_w[i], gnw, wf, wd)
        xp, s_new = _mixer_layer(xp, bp, lp, mods_i, ctx_row, *common, dft_ctx, None, True, fnw, final)
        states.append(s_new)
        xs, _ = _mixer_layer(xs, bs, ls, mods_i, None, *common, dft_lat, state_ctx[:, i], False, fnw, final)
    y_prompt = xp.reshape(bp, lp, D_MODEL)
    y_sample = xs.reshape(bs, ls, D_MODEL)
    state_new = jnp.stack(states, axis=1).astype(x_prompt.dtype)
    return (y_prompt, y_sample, state_new)
```

```python
import functools
import math

import numpy as np
import jax
import jax.numpy as jnp
from jax import lax
from jax.experimental import pallas as pl
from jax.experimental.pallas import tpu as pltpu

D_MODEL = 2048
GRID_W = 64
D_FOURIER = 512
N_FOURIER_GROUPS = 4
FOURIER_GROUP = 128
D_DELTA = 1536
HEAD_DIM = 128
N_HEADS = 12
N_DIR = 2
CONV_K = 3
CHUNK = 128
EPS = 1e-6

D_MAIN = 2 * D_FOURIER + 4 * D_DELTA
N_GATE = 2 * N_DIR * N_HEADS
LANES = 128
COL_Q = (2 * D_FOURIER) // LANES
COL_K = COL_Q + N_HEADS
COL_V = COL_K + N_HEADS
COL_ZD = COL_V + N_HEADS
VMEM_LIMIT = 56 * 1024 * 1024

BF16 = jnp.bfloat16
F32 = jnp.float32
NT_DIMS = (((1,), (1,)), ((), ()))


def _mm(a, b):
    return jnp.dot(a.astype(BF16), b.astype(BF16), preferred_element_type=F32)


def _sigmoid(x):
    return 0.5 * (1.0 + jnp.tanh(0.5 * x))


def _silu(x):
    return x * _sigmoid(x)


def _mod_kernel(c_ref, w_ref, b_ref, o_ref):
    s = _silu(c_ref[...])
    o_ref[0] = _mm(s, w_ref[0]) + b_ref[0]


def _modulation(cvec, w_mod, b_mod):
    depth, _, n = w_mod.shape
    tn = 512
    return pl.pallas_call(
        _mod_kernel,
        grid=(depth, n // tn),
        in_specs=[pl.BlockSpec((8, D_MODEL), lambda l, j: (0, 0)),
                  pl.BlockSpec((1, D_MODEL, tn), lambda l, j: (l, 0, j)),
                  pl.BlockSpec((1, 1, tn), lambda l, j: (l, 0, j))],
        out_specs=pl.BlockSpec((1, 8, tn), lambda l, j: (l, 0, j)),
        out_shape=jax.ShapeDtypeStruct((depth, 8, n), F32),
        compiler_params=pltpu.CompilerParams(dimension_semantics=("arbitrary", "arbitrary")),
    )(cvec, w_mod, b_mod.reshape(depth, 1, n))


def _chunk_scans(lg, axis):
    n = lg.shape[axis]
    pos = lax.broadcasted_iota(jnp.int32, lg.shape, axis) % CHUNK
    pre, suf = lg, lg
    s = 1
    while s < CHUNK:
        pre = pre + jnp.where(pos >= s, pltpu.roll(pre, s, axis), 0.0)
        suf = suf + jnp.where(pos < CHUNK - s, pltpu.roll(suf, n - s, axis), 0.0)
        s *= 2
    return pre, suf


def _gate_tables(ab, a_log, dt_bias, gate_axis):
    tok_axis = 1 - gate_axis
    col = lax.broadcasted_iota(jnp.int32, ab.shape, gate_axis)
    xa = ab + dt_bias
    softplus = jnp.maximum(xa, 0.0) + jnp.log1p(jnp.exp(-jnp.abs(xa)))
    lg = jnp.where(col < N_DIR * N_HEADS, -jnp.exp(a_log) * softplus, 0.0)
    beta = _sigmoid(ab)
    pre, suf = _chunk_scans(lg, tok_axis)
    fwd = col < N_HEADS
    dec = col < N_DIR * N_HEADS
    incl = jnp.where(fwd, pre, jnp.where(dec, suf, jnp.where(col < N_GATE, beta, 0.0)))
    rest = jnp.where(fwd, suf - lg, jnp.where(dec, pre - lg, 0.0))
    return incl, rest


def _inproj_kernel(x_ref, shift_ref, scale_ref, nw_ref, w_ref, wab_ref, wabt_ref,
                   alog_r_ref, dt_r_ref, alog_c_ref, dt_c_ref,
                   proj_ref, gi_ref, gr_ref, git_ref, h_scr, *, slab):
    j = pl.program_id(1)

    @pl.when(j == 0)
    def _():
        tm = x_ref.shape[0]
        for s in range(tm // slab):
            rows = pl.ds(s * slab, slab)
            xf = x_ref[rows, :]
            ms = jnp.mean(xf * xf, axis=-1, keepdims=True)
            y = xf * lax.rsqrt(ms + EPS) * nw_ref[...]
            hb = (y * (1.0 + scale_ref[0]) + shift_ref[0]).astype(BF16)
            h_scr[rows, :] = hb
            ab = jnp.dot(hb, wab_ref[...], preferred_element_type=F32)
            gi, gr = _gate_tables(ab, alog_r_ref[...], dt_r_ref[...], 1)
            gi_ref[rows, :] = gi
            gr_ref[rows, :] = gr
            abt = lax.dot_general(wabt_ref[...], hb, NT_DIMS, preferred_element_type=F32)
            git, _ = _gate_tables(abt, alog_c_ref[...], dt_c_ref[...], 0)
            git_ref[:, s * slab:(s + 1) * slab] = git

    proj_ref[...] = jnp.dot(h_scr[...], w_ref[0], preferred_element_type=F32)


def _in_projection(x2, mods, mod_row, norm_w, w_in, layer, wab, wabt, alog_r, dt_r, alog_c, dt_c, seq_len):
    m = x2.shape[0]
    tm, tn, slab = 1024, 512, 256
    assert m % tm == 0 and D_MAIN % tn == 0 and (seq_len % tm == 0 or tm % seq_len == 0)
    if mod_row is None:
        row = lambda i: (i * tm) // seq_len
    else:
        row = lambda i: mod_row
    const = lambda i, j: (0, 0)
    return pl.pallas_call(
        functools.partial(_inproj_kernel, slab=slab),
        grid=(m // tm, D_MAIN // tn),
        in_specs=[pl.BlockSpec((tm, D_MODEL), lambda i, j: (i, 0)),
                  pl.BlockSpec((1, 1, D_MODEL), lambda i, j: (row(i), 0, 0)),
                  pl.BlockSpec((1, 1, D_MODEL), lambda i, j: (row(i), 0, 1)),
                  pl.BlockSpec((1, D_MODEL), const),
                  pl.BlockSpec((1, D_MODEL, tn), lambda i, j: (layer, 0, j)),
                  pl.BlockSpec((D_MODEL, LANES), const),
                  pl.BlockSpec((LANES, D_MODEL), const),
                  pl.BlockSpec((1, LANES), const),
                  pl.BlockSpec((1, LANES), const),
                  pl.BlockSpec((LANES, 1), const),
                  pl.BlockSpec((LANES, 1), const)],
        out_specs=[pl.BlockSpec((tm, tn), lambda i, j: (i, j)),
                   pl.BlockSpec((tm, LANES), lambda i, j: (i, 0)),
                   pl.BlockSpec((tm, LANES), lambda i, j: (i, 0)),
                   pl.BlockSpec((LANES, tm), lambda i, j: (0, i))],
        out_shape=[jax.ShapeDtypeStruct((m, D_MAIN), F32),
                   jax.ShapeDtypeStruct((m, LANES), F32),
                   jax.ShapeDtypeStruct((m, LANES), F32),
                   jax.ShapeDtypeStruct((LANES, m), F32)],
        scratch_shapes=[pltpu.VMEM((tm, D_MODEL), BF16)],
        compiler_params=pltpu.CompilerParams(dimension_semantics=("arbitrary", "arbitrary"),
                                             vmem_limit_bytes=VMEM_LIMIT),
    )(x2, mods, mods, norm_w, w_in, wab, wabt, alog_r, dt_r, alog_c, dt_c)


def _fourier_kernel(u_ref, z_ref, cs_ref, pm_ref, o_ref, z_scr):
    t = pl.program_id(1)
    seq = u_ref.shape[1]

    @pl.when(t == 0)
    def _():
        xcs = _mm(u_ref[0], cs_ref[...])
        z_scr[0:seq, :] = xcs[:, :D_FOURIER].astype(BF16)
        z_scr[seq:2 * seq, :] = xcs[:, D_FOURIER:].astype(BF16)

    y = jnp.dot(pm_ref[...], z_scr[...], preferred_element_type=F32)
    o_ref[0] = (y * _silu(z_ref[0])).astype(BF16)


def _fourier_mixer(proj3, cs, pm):
    b, seq, _ = proj3.shape
    tl = min(seq, 512)
    return pl.pallas_call(
        _fourier_kernel,
        grid=(b, seq // tl),
        in_specs=[pl.BlockSpec((1, seq, D_FOURIER), lambda i, t: (i, 0, 0)),
                  pl.BlockSpec((1, tl, D_FOURIER), lambda i, t: (i, t, 1)),
                  pl.BlockSpec((D_FOURIER, 2 * D_FOURIER), lambda i, t: (0, 0)),
                  pl.BlockSpec((tl, 2 * seq), lambda i, t: (t, 0))],
        out_specs=pl.BlockSpec((1, tl, D_FOURIER), lambda i, t: (i, t, 0)),
        out_shape=jax.ShapeDtypeStruct((b, seq, D_FOURIER), BF16),
        scratch_shapes=[pltpu.VMEM((2 * seq, D_FOURIER), BF16)],
        compiler_params=pltpu.CompilerParams(dimension_semantics=("arbitrary", "arbitrary"),
                                             vmem_limit_bytes=VMEM_LIMIT),
    )(proj3, proj3, cs, pm)


def _dft_tables(seq, grid):
    def cs(n):
        k = np.arange(n)
        ang = 2.0 * np.pi * ((k[:, None] * k[None, :]) % n) / n
        return np.cos(ang), np.sin(ang)

    cc, sc = cs(FOURIER_GROUP)
    eye = np.eye(N_FOURIER_GROUPS)
    chan = np.concatenate([np.kron(eye, cc), np.kron(eye, sc)], axis=1) / math.sqrt(FOURIER_GROUP)
    if grid:
        cr, sr = cs(seq // GRID_W)
        cw, sw = cs(GRID_W)
        cp = np.kron(cr, cw) - np.kron(sr, sw)
        sp = np.kron(sr, cw) + np.kron(cr, sw)
    else:
        cp, sp = cs(seq)
    pos = np.concatenate([cp, -sp], axis=1) / math.sqrt(seq)
    return jnp.asarray(chan, F32).astype(BF16), jnp.asarray(pos, F32).astype(BF16)


def _delta_kernel(*refs, seq, heads, unroll, has_state_in, has_state_out):
    it = iter(refs)
    take = lambda n: [next(it) for _ in range(n)]
    q_refs, k_refs, v_refs, zd_refs = take(heads), take(heads), take(heads), take(heads)
    cwq_refs, cwk_refs, cwv_refs = take(heads), take(heads), take(heads)
    gi_ref, gr_ref = next(it), next(it)
    gt_refs = [take(N_DIR) for _ in range(heads)]
    gnw_ref = next(it)
    s0_ref = next(it) if has_state_in else None
    y_ref = next(it)
    st_ref = next(it) if has_state_out else None
    qn_s, kn_s, vv_s, u_s, w_s, qg_s, kdt_s, at_s, egl_s, o_s, s_s = take(11)

    head0 = pl.program_id(1) * heads
    n_chunk = seq // CHUNK

    row = lax.broadcasted_iota(jnp.int32, (CHUNK, HEAD_DIM), 0)

    def conv_tile(src_ref, cw_ref, r0, first, last):
        cur = src_ref[0, pl.ds(r0, CHUNK), :]
        lo = pl.multiple_of(jnp.maximum(r0 - 8, 0), 8)
        hi = pl.multiple_of(jnp.minimum(r0 + CHUNK, seq - 8), 8)
        before = src_ref[0, pl.ds(lo, 8), :][7:8, :]
        after = src_ref[0, pl.ds(hi, 8), :][0:1, :]
        before = jnp.where(first, 0.0, before)
        after = jnp.where(last, 0.0, after)
        prev = jnp.where(row == 0, before, pltpu.roll(cur, 1, 0))
        nxt = jnp.where(row == CHUNK - 1, after, pltpu.roll(cur, CHUNK - 1, 0))
        cw = cw_ref[...]
        y = prev * cw[0:1, :] + cur * cw[1:2, :] + nxt * cw[2:3, :]
        return _silu(y)

    def l2n(x):
        return x * lax.rsqrt(jnp.sum(x * x, axis=-1, keepdims=True) + EPS)

    def prep(m, carry):
        r0 = pl.multiple_of(m * CHUNK, CHUNK)
        first, last = m == 0, m == n_chunk - 1
        rows = pl.ds(r0, CHUNK)
        for g in range(heads):
            qn_s[g, rows, :] = l2n(conv_tile(q_refs[g], cwq_refs[g], r0, first, last)) * (HEAD_DIM ** -0.5)
            kn_s[g, rows, :] = l2n(conv_tile(k_refs[g], cwk_refs[g], r0, first, last))
            vv_s[g, rows, :] = conv_tile(v_refs[g], cwv_refs[g], r0, first, last)
        return carry

    lax.fori_loop(0, n_chunk, prep, 0)

    ri = lax.broadcasted_iota(jnp.int32, (CHUNK, CHUNK), 0)
    ci = lax.broadcasted_iota(jnp.int32, (CHUNK, CHUNK), 1)
    eye = (ri == ci).astype(F32)
    lane = lax.broadcasted_iota(jnp.int32, (CHUNK, LANES), 1)

    def pick(tile, c):
        return jnp.sum(jnp.where(lane == c, tile, 0.0), axis=-1, keepdims=True)

    def local(t, carry):
        tiles = []
        for g in range(heads):
            for k in range(unroll):
                m = t * unroll + k
                rows = pl.ds(pl.multiple_of(m * CHUNK, CHUNK), CHUNK)
                qn, kn, vv = qn_s[g, rows, :], kn_s[g, rows, :], vv_s[g, rows, :]
                kb16 = kn.astype(BF16)
                gram = lax.dot_general(jnp.concatenate([kb16, qn.astype(BF16)], axis=0), kb16, NT_DIMS,
                                       preferred_element_type=F32)
                tiles.append((g, m, rows, qn, kn, vv, gram[:CHUNK], gram[CHUNK:]))
        chains = []
        for g, m, rows, qn, kn, vv, kk, qk in tiles:
            gi_t, gr_t = gi_ref[0, rows, :], gr_ref[0, rows, :]
            for d in range(N_DIR):
                c_g = head0 + g + d * N_HEADS
                g_col = pick(gi_t, c_g)
                b_col = pick(gi_t, c_g + N_DIR * N_HEADS)
                r_col = pick(gr_t, c_g)
                g_row = gt_refs[g][d][0, 0, pl.ds(m, 1), :]
                incl = (ri >= ci) if d == 0 else (ri <= ci)
                decay = jnp.where(incl, jnp.exp(jnp.where(incl, g_col - g_row, 0.0)), 0.0)
                lmat = jnp.where(ri == ci, 0.0, b_col * kk * decay)
                at_s[g, d, m] = (qk * decay).astype(BF16)
                e_g = jnp.exp(g_col)
                qg_s[g, d, rows, :] = (qn * e_g).astype(BF16)
                kdt_s[g, d, m] = jnp.transpose(kn * jnp.exp(r_col)).astype(BF16)
                total = g_col[0:1, :] + r_col[0:1, :]
                egl_s[g, d, pl.ds(m, 1), :] = jnp.broadcast_to(jnp.exp(total), (1, LANES))
                rhs = jnp.concatenate([vv * b_col, kn * b_col * e_g], axis=1).astype(BF16)
                chains.append((g, d, rows, lmat, rhs))
        tinvs = [eye - lmat for _, _, _, lmat, _ in chains]
        powers = [lmat.astype(BF16) for _, _, _, lmat, _ in chains]
        powers = [jnp.dot(p, p, preferred_element_type=F32).astype(BF16) for p in powers]
        for _ in range(5):
            prods = [jnp.dot(jnp.concatenate([ti.astype(BF16), p], axis=0), p, preferred_element_type=F32)
                     for ti, p in zip(tinvs, powers)]
            tinvs = [ti + pr[:CHUNK] for ti, pr in zip(tinvs, prods)]
            powers = [pr[CHUNK:].astype(BF16) for pr in prods]
        tinvs = [ti + jnp.dot(ti.astype(BF16), p, preferred_element_type=F32) for ti, p in zip(tinvs, powers)]
        uws = [jnp.dot(ti.astype(BF16), rhs, preferred_element_type=F32)
               for ti, (_, _, _, _, rhs) in zip(tinvs, chains)]
        for uw, (g, d, rows, _, _) in zip(uws, chains):
            u_s[g, d, rows, :] = uw[:, :HEAD_DIM]
            w_s[g, d, rows, :] = uw[:, HEAD_DIM:].astype(BF16)
        return carry

    lax.fori_loop(0, n_chunk // unroll, local, 0)

    for g in range(heads):
        for d in range(N_DIR):
            s_s[g, d] = s0_ref[0, 0, d, g] if has_state_in else jnp.zeros((HEAD_DIM, HEAD_DIM), F32)

    def step(n, carry):
        chains = [(g, d) for g in range(heads) for d in range(N_DIR)]
        where = []
        for g, d in chains:
            c = n if d == 0 else n_chunk - 1 - n
            where.append((c, pl.ds(pl.multiple_of(c * CHUNK, CHUNK), CHUNK)))
        s_old = [s_s[g, d] for g, d in chains]
        sws = [jnp.dot(jnp.concatenate([w_s[g, d, rows, :], qg_s[g, d, rows, :]], axis=0),
                       so.astype(BF16), preferred_element_type=F32)
               for (g, d), (c, rows), so in zip(chains, where, s_old)]
        vns = [(u_s[g, d, rows, :] - sw[:CHUNK]).astype(BF16)
               for (g, d), (c, rows), sw in zip(chains, where, sws)]
        outs = [jnp.dot(jnp.concatenate([at_s[g, d, c], kdt_s[g, d, c]], axis=0), vn,
                        preferred_element_type=F32)
                for (g, d), (c, rows), vn in zip(chains, where, vns)]
        for (g, d), (c, rows), so, sw, out in zip(chains, where, s_old, sws, outs):
            o_s[g, d, rows, :] = sw[CHUNK:] + out[:CHUNK]
            s_s[g, d] = so * egl_s[g, d, pl.ds(c, 1), :] + out[CHUNK:]
        return carry

    lax.fori_loop(0, n_chunk, step, 0)

    if has_state_out:
        for g in range(heads):
            for d in range(N_DIR):
                st_ref[0, d, g] = s_s[g, d]

    def finish(m, carry):
        rows = pl.ds(pl.multiple_of(m * CHUNK, CHUNK), CHUNK)
        for g in range(heads):
            o = o_s[g, 0, rows, :] + o_s[g, 1, rows, :]
            y = o * lax.rsqrt(jnp.mean(o * o, axis=-1, keepdims=True) + EPS) * gnw_ref[...]
            y_ref[0, rows, g * HEAD_DIM:(g + 1) * HEAD_DIM] = (y * _silu(zd_refs[g][0, rows, :])).astype(BF16)
        return carry

    lax.fori_loop(0, n_chunk, finish, 0)


def _delta_mixer(proj3, conv_w, gi3, gr3, git4, gnorm_w, s0, layer, want_state, heads, unroll):
    b, seq, _ = proj3.shape
    n_chunk = seq // CHUNK
    assert N_HEADS % heads == 0 and n_chunk % unroll == 0
    slots = range(heads)
    tok = lambda col0: [pl.BlockSpec((1, seq, HEAD_DIM), lambda i, h, g=g: (i, 0, col0 + h * heads + g))
                        for g in slots]
    cw = lambda col0: [pl.BlockSpec((CONV_K, HEAD_DIM), lambda i, h, g=g: (0, col0 + h * heads + g))
                       for g in slots]
    gate_rows = [pl.BlockSpec((1, 1, n_chunk, LANES), lambda i, h, g=g, d=d: (d * N_HEADS + h * heads + g, i, 0, 0))
                 for g in slots for d in range(N_DIR)]
    in_specs = (tok(COL_Q) + tok(COL_K) + tok(COL_V) + tok(COL_ZD) + cw(0) + cw(N_HEADS) + cw(2 * N_HEADS)
                + [pl.BlockSpec((1, seq, LANES), lambda i, h: (i, 0, 0)),
                   pl.BlockSpec((1, seq, LANES), lambda i, h: (i, 0, 0))]
                + gate_rows + [pl.BlockSpec((1, HEAD_DIM), lambda i, h: (0, 0))])
    args = [proj3] * (4 * heads) + [conv_w] * (3 * heads) + [gi3, gr3] + [git4] * (N_DIR * heads) + [gnorm_w]
    state_spec = pl.BlockSpec((1, N_DIR, heads, HEAD_DIM, HEAD_DIM), lambda i, h: (i, 0, h, 0, 0))
    if s0 is not None:
        in_specs.append(pl.BlockSpec((1, 1, N_DIR, heads, HEAD_DIM, HEAD_DIM),
                                     lambda i, h: (i, layer, 0, h, 0, 0)))
        args.append(s0)
    out_specs = [pl.BlockSpec((1, seq, heads * HEAD_DIM), lambda i, h: (i, 0, h))]
    out_shape = [jax.ShapeDtypeStruct((b, seq, D_DELTA), BF16)]
    if want_state:
        out_specs.append(state_spec)
        out_shape.append(jax.ShapeDtypeStruct((b, N_DIR, N_HEADS, HEAD_DIM, HEAD_DIM), F32))
    scratch = [pltpu.VMEM((heads, seq, HEAD_DIM), F32),
               pltpu.VMEM((heads, seq, HEAD_DIM), F32),
               pltpu.VMEM((heads, seq, HEAD_DIM), F32),
               pltpu.VMEM((heads, N_DIR, seq, HEAD_DIM), F32),
               pltpu.VMEM((heads, N_DIR, seq, HEAD_DIM), BF16),
               pltpu.VMEM((heads, N_DIR, seq, HEAD_DIM), BF16),
               pltpu.VMEM((heads, N_DIR, n_chunk, HEAD_DIM, CHUNK), BF16),
               pltpu.VMEM((heads, N_DIR, n_chunk, CHUNK, CHUNK), BF16),
               pltpu.VMEM((heads, N_DIR, n_chunk, LANES), F32),
               pltpu.VMEM((heads, N_DIR, seq, HEAD_DIM), F32),
               pltpu.VMEM((heads, N_DIR, HEAD_DIM, HEAD_DIM), F32)]
    res = pl.pallas_call(
        functools.partial(_delta_kernel, seq=seq, heads=heads, unroll=unroll,
                          has_state_in=s0 is not None, has_state_out=want_state),
        grid=(b, N_HEADS // heads),
        in_specs=in_specs,
        out_specs=out_specs,
        out_shape=out_shape,
        scratch_shapes=scratch,
        compiler_params=pltpu.CompilerParams(dimension_semantics=("arbitrary", "arbitrary"),
                                             vmem_limit_bytes=VMEM_LIMIT),
    )(*args)
    return (res[0], res[1]) if want_state else (res[0], None)


def _outproj_kernel(yf_ref, yd_ref, w_ref, x_ref, gate_ref, fnw_ref, o_ref, *, final):
    y = jnp.dot(yf_ref[...], w_ref[0, :D_FOURIER, :], preferred_element_type=F32)
    y = y + jnp.dot(yd_ref[...], w_ref[0, D_FOURIER:, :], preferred_element_type=F32)
    x_new = x_ref[...] + gate_ref[0] * y
    if final:
        ms = jnp.mean(x_new * x_new, axis=-1, keepdims=True)
        x_new = x_new * lax.rsqrt(ms + EPS) * fnw_ref[...]
    o_ref[...] = x_new


def _out_projection(yf2, yd2, w_out, layer, x2, mods, mod_row, final_norm_w, seq_len, final):
    m = x2.shape[0]
    tm = 256
    assert m % tm == 0 and seq_len % tm == 0
    if mod_row is None:
        row = lambda i: (i * tm) // seq_len
    else:
        row = lambda i: mod_row
    return pl.pallas_call(
        functools.partial(_outproj_kernel, final=final),
        grid=(m // tm,),
        in_specs=[pl.BlockSpec((tm, D_FOURIER), lambda i: (i, 0)),
                  pl.BlockSpec((tm, D_DELTA), lambda i: (i, 0)),
                  pl.BlockSpec((1, D_MODEL, D_MODEL), lambda i: (layer, 0, 0)),
                  pl.BlockSpec((tm, D_MODEL), lambda i: (i, 0)),
                  pl.BlockSpec((1, 1, D_MODEL), lambda i: (row(i), 0, 2)),
                  pl.BlockSpec((1, D_MODEL), lambda i: (0, 0))],
        out_specs=pl.BlockSpec((tm, D_MODEL), lambda i: (i, 0)),
        out_shape=jax.ShapeDtypeStruct((m, D_MODEL), F32),
        compiler_params=pltpu.CompilerParams(dimension_semantics=("arbitrary",),
                                             vmem_limit_bytes=VMEM_LIMIT),
    )(yf2, yd2, w_out, x2, mods, final_norm_w)


def _mixer_layer(x2, batch, seq, mods, mod_row, norm_w, w_in, layer, wab, wabt, gate_params, conv_w,
                 gnorm_w, w_out, dft, s0, want_state, final_norm_w, final):
    alog_r, dt_r, alog_c, dt_c = gate_params
    proj, gi, gr, git = _in_projection(x2, mods, mod_row, norm_w, w_in, layer, wab, wabt,
                                       alog_r, dt_r, alog_c, dt_c, seq)
    proj3 = proj.reshape(batch, seq, D_MAIN)
    yf = _fourier_mixer(proj3, *dft)
    yd, state = _delta_mixer(proj3, conv_w, gi.reshape(batch, seq, LANES), gr.reshape(batch, seq, LANES),
                             git.reshape(LANES, batch, seq // CHUNK, LANES), gnorm_w, s0, layer, want_state,
                             heads=2 if seq > 4 * CHUNK else 4, unroll=4 if seq > 4 * CHUNK else 2)
    x_new = _out_projection(yf.reshape(batch * seq, D_FOURIER), yd.reshape(batch * seq, D_DELTA),
                            w_out, layer, x2, mods, mod_row, final_norm_w, seq, final)
    return x_new, state


def kernel(x_prompt, x_sample, state_ctx, c, c_ctx, norm_w, w_mod, b_mod, w_in, conv_w, a_log, dt_bias,
           gnorm_w, w_out, final_norm_w):
    depth = w_in.shape[0]
    bp, lp, _ = x_prompt.shape
    bs, ls, _ = x_sample.shape
    ctx_row = bs

    cvec = jnp.zeros((8, D_MODEL), F32).at[:bs].set(c).at[ctx_row].set(c_ctx)
    mods = _modulation(cvec, w_mod, b_mod)
    dft_ctx = _dft_tables(lp, False)
    dft_lat = _dft_tables(ls, True)
    fnw = final_norm_w.reshape(1, D_MODEL)
    w_in_bf = w_in.astype(BF16)
    w_out_bf = w_out.astype(BF16)

    xp = x_prompt.reshape(bp * lp, D_MODEL)
    xs = x_sample.reshape(bs * ls, D_MODEL)
    states = []
    for i in range(depth):
        w_gate = jnp.pad(w_in[i][:, D_MAIN:], ((0, 0), (0, LANES - N_GATE)))
        wab = w_gate.astype(BF16)
        wabt = w_gate.T.astype(BF16)
        pad = (0, LANES - N_DIR * N_HEADS)
        alog = jnp.pad(a_log[i].reshape(-1), pad)
        dtb = jnp.pad(dt_bias[i].reshape(-1), pad)
        gate_params = (alog.reshape(1, LANES), dtb.reshape(1, LANES),
                       alog.reshape(LANES, 1), dtb.reshape(LANES, 1))
        mods_i = mods[i].reshape(8, 1, 3 * D_MODEL)
        nw = norm_w[i].reshape(1, D_MODEL)
        gnw = gnorm_w[i].reshape(1, HEAD_DIM)
        final = i == depth - 1
        common = (nw, w_in_bf, i, wab, wabt, gate_params, conv_w[i], gnw, w_out_bf)
        xp, s_new = _mixer_layer(xp, bp, lp, mods_i, ctx_row, *common, dft_ctx, None, True, fnw, final)
        states.append(s_new)
        xs, _ = _mixer_layer(xs, bs, ls, mods_i, None, *common, dft_lat, state_ctx, False, fnw, final)
    y_prompt = xp.reshape(bp, lp, D_MODEL)
    y_sample = xs.reshape(bs, ls, D_MODEL)
    state_new = jnp.stack(states, axis=1).astype(x_prompt.dtype)
    return (y_prompt, y_sample, state_new)
```

```python
import functools
import math

import numpy as np
import jax
import jax.numpy as jnp
from jax import lax
from jax.experimental import pallas as pl
from jax.experimental.pallas import tpu as pltpu

D_MODEL = 2048
GRID_W = 64
D_FOURIER = 512
N_FOURIER_GROUPS = 4
FOURIER_GROUP = 128
D_DELTA = 1536
HEAD_DIM = 128
N_HEADS = 12
N_DIR = 2
CONV_K = 3
CHUNK = 128
EPS = 1e-6

D_MAIN = 2 * D_FOURIER + 4 * D_DELTA
N_GATE = 2 * N_DIR * N_HEADS
LANES = 128
COL_Q = (2 * D_FOURIER) // LANES
COL_K = COL_Q + N_HEADS
COL_V = COL_K + N_HEADS
COL_ZD = COL_V + N_HEADS
VMEM_LIMIT = 56 * 1024 * 1024

BF16 = jnp.bfloat16
F32 = jnp.float32
NT_DIMS = (((1,), (1,)), ((), ()))


def _mm(a, b):
    return jnp.dot(a.astype(BF16), b.astype(BF16), preferred_element_type=F32)


def _sigmoid(x):
    return 0.5 * (1.0 + jnp.tanh(0.5 * x))


def _silu(x):
    return x * _sigmoid(x)


def _mod_kernel(c_ref, w_ref, b_ref, o_ref):
    s = _silu(c_ref[...])
    o_ref[0] = _mm(s, w_ref[0]) + b_ref[0]


def _modulation(cvec, w_mod, b_mod):
    depth, _, n = w_mod.shape
    tn = 512
    return pl.pallas_call(
        _mod_kernel,
        grid=(depth, n // tn),
        in_specs=[pl.BlockSpec((8, D_MODEL), lambda l, j: (0, 0)),
                  pl.BlockSpec((1, D_MODEL, tn), lambda l, j: (l, 0, j)),
                  pl.BlockSpec((1, 1, tn), lambda l, j: (l, 0, j))],
        out_specs=pl.BlockSpec((1, 8, tn), lambda l, j: (l, 0, j)),
        out_shape=jax.ShapeDtypeStruct((depth, 8, n), F32),
        compiler_params=pltpu.CompilerParams(dimension_semantics=("arbitrary", "arbitrary")),
    )(cvec, w_mod, b_mod.reshape(depth, 1, n))


def _chunk_scans(lg, axis):
    n = lg.shape[axis]
    pos = lax.broadcasted_iota(jnp.int32, lg.shape, axis) % CHUNK
    pre, suf = lg, lg
    s = 1
    while s < CHUNK:
        pre = pre + jnp.where(pos >= s, pltpu.roll(pre, s, axis), 0.0)
        suf = suf + jnp.where(pos < CHUNK - s, pltpu.roll(suf, n - s, axis), 0.0)
        s *= 2
    return pre, suf


def _gate_tables(ab, a_log, dt_bias, gate_axis):
    tok_axis = 1 - gate_axis
    col = lax.broadcasted_iota(jnp.int32, ab.shape, gate_axis)
    xa = ab + dt_bias
    softplus = jnp.maximum(xa, 0.0) + jnp.log1p(jnp.exp(-jnp.abs(xa)))
    lg = jnp.where(col < N_DIR * N_HEADS, -jnp.exp(a_log) * softplus, 0.0)
    beta = _sigmoid(ab)
    pre, suf = _chunk_scans(lg, tok_axis)
    fwd = col < N_HEADS
    dec = col < N_DIR * N_HEADS
    incl = jnp.where(fwd, pre, jnp.where(dec, suf, jnp.where(col < N_GATE, beta, 0.0)))
    rest = jnp.where(fwd, suf - lg, jnp.where(dec, pre - lg, 0.0))
    return incl, rest


def _inproj_kernel(x_ref, shift_ref, scale_ref, nw_ref, w_ref, wab_ref, wabt_ref,
                   alog_r_ref, dt_r_ref, alog_c_ref, dt_c_ref,
                   proj_ref, gi_ref, gr_ref, git_ref, h_scr, *, slab):
    j = pl.program_id(1)

    @pl.when(j == 0)
    def _():
        tm = x_ref.shape[0]
        for s in range(tm // slab):
            rows = pl.ds(s * slab, slab)
            xf = x_ref[rows, :]
            ms = jnp.mean(xf * xf, axis=-1, keepdims=True)
            y = xf * lax.rsqrt(ms + EPS) * nw_ref[...]
            hb = (y * (1.0 + scale_ref[0]) + shift_ref[0]).astype(BF16)
            h_scr[rows, :] = hb
            ab = jnp.dot(hb, wab_ref[...], preferred_element_type=F32)
            gi, gr = _gate_tables(ab, alog_r_ref[...], dt_r_ref[...], 1)
            gi_ref[rows, :] = gi
            gr_ref[rows, :] = gr
            abt = lax.dot_general(wabt_ref[...], hb, NT_DIMS, preferred_element_type=F32)
            git, _ = _gate_tables(abt, alog_c_ref[...], dt_c_ref[...], 0)
            git_ref[:, s * slab:(s + 1) * slab] = git

    proj_ref[...] = jnp.dot(h_scr[...], w_ref[0], preferred_element_type=F32)


def _in_projection(x2, mods, mod_row, norm_w, w_in, layer, wab, wabt, alog_r, dt_r, alog_c, dt_c, seq_len):
    m = x2.shape[0]
    tm, tn, slab = 1024, 512, 256
    assert m % tm == 0 and D_MAIN % tn == 0 and (seq_len % tm == 0 or tm % seq_len == 0)
    if mod_row is None:
        row = lambda i: (i * tm) // seq_len
    else:
        row = lambda i: mod_row
    const = lambda i, j: (0, 0)
    return pl.pallas_call(
        functools.partial(_inproj_kernel, slab=slab),
        grid=(m // tm, D_MAIN // tn),
        in_specs=[pl.BlockSpec((tm, D_MODEL), lambda i, j: (i, 0)),
                  pl.BlockSpec((1, 1, D_MODEL), lambda i, j: (row(i), 0, 0)),
                  pl.BlockSpec((1, 1, D_MODEL), lambda i, j: (row(i), 0, 1)),
                  pl.BlockSpec((1, D_MODEL), const),
                  pl.BlockSpec((1, D_MODEL, tn), lambda i, j: (layer, 0, j)),
                  pl.BlockSpec((D_MODEL, LANES), const),
                  pl.BlockSpec((LANES, D_MODEL), const),
                  pl.BlockSpec((1, LANES), const),
                  pl.BlockSpec((1, LANES), const),
                  pl.BlockSpec((LANES, 1), const),
                  pl.BlockSpec((LANES, 1), const)],
        out_specs=[pl.BlockSpec((tm, tn), lambda i, j: (i, j)),
                   pl.BlockSpec((tm, LANES), lambda i, j: (i, 0)),
                   pl.BlockSpec((tm, LANES), lambda i, j: (i, 0)),
                   pl.BlockSpec((LANES, tm), lambda i, j: (0, i))],
        out_shape=[jax.ShapeDtypeStruct((m, D_MAIN), F32),
                   jax.ShapeDtypeStruct((m, LANES), F32),
                   jax.ShapeDtypeStruct((m, LANES), F32),
                   jax.ShapeDtypeStruct((LANES, m), F32)],
        scratch_shapes=[pltpu.VMEM((tm, D_MODEL), BF16)],
        compiler_params=pltpu.CompilerParams(dimension_semantics=("arbitrary", "arbitrary"),
                                             vmem_limit_bytes=VMEM_LIMIT),
    )(x2, mods, mods, norm_w, w_in, wab, wabt, alog_r, dt_r, alog_c, dt_c)


def _fourier_kernel(u_ref, z_ref, cs_ref, pm_ref, o_ref, z_scr):
    t = pl.program_id(1)
    seq = u_ref.shape[1]

    @pl.when(t == 0)
    def _():
        xcs = _mm(u_ref[0], cs_ref[...])
        z_scr[0:seq, :] = xcs[:, :D_FOURIER].astype(BF16)
        z_scr[seq:2 * seq, :] = xcs[:, D_FOURIER:].astype(BF16)

    y = jnp.dot(pm_ref[...], z_scr[...], preferred_element_type=F32)
    o_ref[0] = (y * _silu(z_ref[0])).astype(BF16)


def _fourier_mixer(proj3, cs, pm):
    b, seq, _ = proj3.shape
    tl = min(seq, 512)
    return pl.pallas_call(
        _fourier_kernel,
        grid=(b, seq // tl),
        in_specs=[pl.BlockSpec((1, seq, D_FOURIER), lambda i, t: (i, 0, 0)),
                  pl.BlockSpec((1, tl, D_FOURIER), lambda i, t: (i, t, 1)),
                  pl.BlockSpec((D_FOURIER, 2 * D_FOURIER), lambda i, t: (0, 0)),
                  pl.BlockSpec((tl, 2 * seq), lambda i, t: (t, 0))],
        out_specs=pl.BlockSpec((1, tl, D_FOURIER), lambda i, t: (i, t, 0)),
        out_shape=jax.ShapeDtypeStruct((b, seq, D_FOURIER), BF16),
        scratch_shapes=[pltpu.VMEM((2 * seq, D_FOURIER), BF16)],
        compiler_params=pltpu.CompilerParams(dimension_semantics=("arbitrary", "arbitrary"),
                                             vmem_limit_bytes=VMEM_LIMIT),
    )(proj3, proj3, cs, pm)


def _dft_tables(seq, grid):
    def cs(n):
        k = np.arange(n)
        ang = 2.0 * np.pi * ((k[:, None] * k[None, :]) % n) / n
        return np.cos(ang), np.sin(ang)

    cc, sc = cs(FOURIER_GROUP)
    eye = np.eye(N_FOURIER_GROUPS)
    chan = np.concatenate([np.kron(eye, cc), np.kron(eye, sc)], axis=1) / math.sqrt(FOURIER_GROUP)
    if grid:
        cr, sr = cs(seq // GRID_W)
        cw, sw = cs(GRID_W)
        cp = np.kron(cr, cw) - np.kron(sr, sw)
        sp = np.kron(sr, cw) + np.kron(cr, sw)
    else:
        cp, sp = cs(seq)
    pos = np.concatenate([cp, -sp], axis=1) / math.sqrt(seq)
    return jnp.asarray(chan, F32).astype(BF16), jnp.asarray(pos, F32).astype(BF16)


def _delta_kernel(*refs, seq, heads, unroll, has_state_in, has_state_out):
    it = iter(refs)
    take = lambda n: [next(it) for _ in range(n)]
    q_refs, k_refs, v_refs, zd_refs = take(heads), take(heads), take(heads), take(heads)
    cwq_refs, cwk_refs, cwv_refs = take(heads), take(heads), take(heads)
    gi_ref, gr_ref = next(it), next(it)
    gt_refs = [take(N_DIR) for _ in range(heads)]
    gnw_ref = next(it)
    s0_ref = next(it) if has_state_in else None
    y_ref = next(it)
    st_ref = next(it) if has_state_out else None
    qn_s, kn_s, vv_s, u_s, w_s, qg_s, kdt_s, at_s, egl_s, o_s, s_s = take(11)

    head0 = pl.program_id(1) * heads
    n_chunk = seq // CHUNK

    row = lax.broadcasted_iota(jnp.int32, (CHUNK, HEAD_DIM), 0)

    def conv_tile(src_ref, cw_ref, r0, first, last):
        cur = src_ref[0, pl.ds(r0, CHUNK), :]
        if isinstance(r0, int):
            lo, hi = max(r0 - 8, 0), min(r0 + CHUNK, seq - 8)
        else:
            lo = pl.multiple_of(jnp.maximum(r0 - 8, 0), 8)
            hi = pl.multiple_of(jnp.minimum(r0 + CHUNK, seq - 8), 8)
        before = src_ref[0, pl.ds(lo, 8), :][7:8, :]
        after = src_ref[0, pl.ds(hi, 8), :][0:1, :]
        before = jnp.where(first, 0.0, before)
        after = jnp.where(last, 0.0, after)
        prev = jnp.where(row == 0, before, pltpu.roll(cur, 1, 0))
        nxt = jnp.where(row == CHUNK - 1, after, pltpu.roll(cur, CHUNK - 1, 0))
        cw = cw_ref[...]
        y = prev * cw[0:1, :] + cur * cw[1:2, :] + nxt * cw[2:3, :]
        return _silu(y)

    def l2n(x):
        return x * lax.rsqrt(jnp.sum(x * x, axis=-1, keepdims=True) + EPS)

    def prep(m):
        r0 = m * CHUNK if isinstance(m, int) else pl.multiple_of(m * CHUNK, CHUNK)
        first, last = m == 0, m == n_chunk - 1
        rows = pl.ds(r0, CHUNK)
        for g in range(heads):
            qn_s[g, rows, :] = l2n(conv_tile(q_refs[g], cwq_refs[g], r0, first, last)) * (HEAD_DIM ** -0.5)
            kn_s[g, rows, :] = l2n(conv_tile(k_refs[g], cwk_refs[g], r0, first, last))
            vv_s[g, rows, :] = conv_tile(v_refs[g], cwv_refs[g], r0, first, last)

    ri = lax.broadcasted_iota(jnp.int32, (CHUNK, CHUNK), 0)
    ci = lax.broadcasted_iota(jnp.int32, (CHUNK, CHUNK), 1)
    eye = (ri == ci).astype(F32)
    lane = lax.broadcasted_iota(jnp.int32, (CHUNK, LANES), 1)

    def pick(tile, c):
        return jnp.sum(jnp.where(lane == c, tile, 0.0), axis=-1, keepdims=True)

    def local(t):
        tiles = []
        for g in range(heads):
            for k in range(unroll):
                m = t * unroll + k
                rows = pl.ds(m * CHUNK if isinstance(m, int) else pl.multiple_of(m * CHUNK, CHUNK), CHUNK)
                qn, kn, vv = qn_s[g, rows, :], kn_s[g, rows, :], vv_s[g, rows, :]
                kb16 = kn.astype(BF16)
                gram = lax.dot_general(jnp.concatenate([kb16, qn.astype(BF16)], axis=0), kb16, NT_DIMS,
                                       preferred_element_type=F32)
                tiles.append((g, m, rows, qn, kn, vv, gram[:CHUNK], gram[CHUNK:]))
        chains = []
        for g, m, rows, qn, kn, vv, kk, qk in tiles:
            gi_t, gr_t = gi_ref[0, rows, :], gr_ref[0, rows, :]
            for d in range(N_DIR):
                c_g = head0 + g + d * N_HEADS
                g_col = pick(gi_t, c_g)
                b_col = pick(gi_t, c_g + N_DIR * N_HEADS)
                r_col = pick(gr_t, c_g)
                g_row = gt_refs[g][d][0, 0, pl.ds(m, 1), :]
                incl = (ri >= ci) if d == 0 else (ri <= ci)
                decay = jnp.where(incl, jnp.exp(jnp.where(incl, g_col - g_row, 0.0)), 0.0)
                lmat = jnp.where(ri == ci, 0.0, b_col * kk * decay)
                at_s[g, d, m] = (qk * decay).astype(BF16)
                e_g = jnp.exp(g_col)
                qg_s[g, d, rows, :] = (qn * e_g).astype(BF16)
                kdt_s[g, d, m] = jnp.transpose(kn * jnp.exp(r_col)).astype(BF16)
                total = g_col[0:1, :] + r_col[0:1, :]
                egl_s[g, d, pl.ds(m, 1), :] = jnp.broadcast_to(jnp.exp(total), (1, LANES))
                rhs = jnp.concatenate([vv * b_col, kn * b_col * e_g], axis=1).astype(BF16)
                chains.append((g, d, rows, lmat, rhs))
        tinvs = [eye - lmat for _, _, _, lmat, _ in chains]
        powers = [lmat.astype(BF16) for _, _, _, lmat, _ in chains]
        powers = [jnp.dot(p, p, preferred_element_type=F32).astype(BF16) for p in powers]
        for _ in range(5):
            prods = [jnp.dot(jnp.concatenate([ti.astype(BF16), p], axis=0), p, preferred_element_type=F32)
                     for ti, p in zip(tinvs, powers)]
            tinvs = [ti + pr[:CHUNK] for ti, pr in zip(tinvs, prods)]
            powers = [pr[CHUNK:].astype(BF16) for pr in prods]
        tinvs = [ti + jnp.dot(ti.astype(BF16), p, preferred_element_type=F32) for ti, p in zip(tinvs, powers)]
        uws = [jnp.dot(ti.astype(BF16), rhs, preferred_element_type=F32)
               for ti, (_, _, _, _, rhs) in zip(tinvs, chains)]
        for uw, (g, d, rows, _, _) in zip(uws, chains):
            u_s[g, d, rows, :] = uw[:, :HEAD_DIM]
            w_s[g, d, rows, :] = uw[:, HEAD_DIM:].astype(BF16)

    n_iter = n_chunk // unroll
    for k in range(unroll):
        prep(k)

    def local_and_next_prep(t, carry):
        for k in range(unroll):
            prep((t + 1) * unroll + k)
        local(t)
        return carry

    lax.fori_loop(0, n_iter - 1, local_and_next_prep, 0)
    local(n_iter - 1)

    for g in range(heads):
        for d in range(N_DIR):
            s_s[g, d] = s0_ref[0, 0, d, g] if has_state_in else jnp.zeros((HEAD_DIM, HEAD_DIM), F32)

    def step(n):
        chains = [(g, d) for g in range(heads) for d in range(N_DIR)]
        where = []
        for g, d in chains:
            c = n if d == 0 else n_chunk - 1 - n
            where.append((c, pl.ds(pl.multiple_of(c * CHUNK, CHUNK), CHUNK)))
        s_old = [s_s[g, d] for g, d in chains]
        sws = [jnp.dot(jnp.concatenate([w_s[g, d, rows, :], qg_s[g, d, rows, :]], axis=0),
                       so.astype(BF16), preferred_element_type=F32)
               for (g, d), (c, rows), so in zip(chains, where, s_old)]
        vns = [(u_s[g, d, rows, :] - sw[:CHUNK]).astype(BF16)
               for (g, d), (c, rows), sw in zip(chains, where, sws)]
        outs = [jnp.dot(jnp.concatenate([at_s[g, d, c], kdt_s[g, d, c]], axis=0), vn,
                        preferred_element_type=F32)
                for (g, d), (c, rows), vn in zip(chains, where, vns)]
        for (g, d), (c, rows), so, sw, out in zip(chains, where, s_old, sws, outs):
            o_s[g, d, rows, :] = sw[CHUNK:] + out[:CHUNK]
            s_s[g, d] = so * egl_s[g, d, pl.ds(c, 1), :] + out[CHUNK:]

    def finish(c):
        rows = pl.ds(c * CHUNK if isinstance(c, int) else pl.multiple_of(c * CHUNK, CHUNK), CHUNK)
        for g in range(heads):
            o = o_s[g, 0, rows, :] + o_s[g, 1, rows, :]
            y = o * lax.rsqrt(jnp.mean(o * o, axis=-1, keepdims=True) + EPS) * gnw_ref[...]
            y_ref[0, rows, g * HEAD_DIM:(g + 1) * HEAD_DIM] = (y * _silu(zd_refs[g][0, rows, :])).astype(BF16)

    half = n_chunk // 2 + 1

    def plain_step(n, carry):
        step(n)
        return carry

    def step_and_finish(n, carry):
        finish(n - 1)
        finish(n_chunk - n)
        step(n)
        return carry

    lax.fori_loop(0, half, plain_step, 0)
    lax.fori_loop(half, n_chunk, step_and_finish, 0)
    finish(n_chunk - 1)
    finish(0)

    if has_state_out:
        for g in range(heads):
            for d in range(N_DIR):
                st_ref[0, d, g] = s_s[g, d]


def _delta_mixer(proj3, conv_w, gi3, gr3, git4, gnorm_w, s0, layer, want_state, heads, unroll):
    b, seq, _ = proj3.shape
    n_chunk = seq // CHUNK
    assert N_HEADS % heads == 0 and n_chunk % unroll == 0
    slots = range(heads)
    tok = lambda col0: [pl.BlockSpec((1, seq, HEAD_DIM), lambda i, h, g=g: (i, 0, col0 + h * heads + g))
                        for g in slots]
    cw = lambda col0: [pl.BlockSpec((CONV_K, HEAD_DIM), lambda i, h, g=g: (0, col0 + h * heads + g))
                       for g in slots]
    gate_rows = [pl.BlockSpec((1, 1, n_chunk, LANES), lambda i, h, g=g, d=d: (d * N_HEADS + h * heads + g, i, 0, 0))
                 for g in slots for d in range(N_DIR)]
    in_specs = (tok(COL_Q) + tok(COL_K) + tok(COL_V) + tok(COL_ZD) + cw(0) + cw(N_HEADS) + cw(2 * N_HEADS)
                + [pl.BlockSpec((1, seq, LANES), lambda i, h: (i, 0, 0)),
                   pl.BlockSpec((1, seq, LANES), lambda i, h: (i, 0, 0))]
                + gate_rows + [pl.BlockSpec((1, HEAD_DIM), lambda i, h: (0, 0))])
    args = [proj3] * (4 * heads) + [conv_w] * (3 * heads) + [gi3, gr3] + [git4] * (N_DIR * heads) + [gnorm_w]
    state_spec = pl.BlockSpec((1, N_DIR, heads, HEAD_DIM, HEAD_DIM), lambda i, h: (i, 0, h, 0, 0))
    if s0 is not None:
        in_specs.append(pl.BlockSpec((1, 1, N_DIR, heads, HEAD_DIM, HEAD_DIM),
                                     lambda i, h: (i, layer, 0, h, 0, 0)))
        args.append(s0)
    out_specs = [pl.BlockSpec((1, seq, heads * HEAD_DIM), lambda i, h: (i, 0, h))]
    out_shape = [jax.ShapeDtypeStruct((b, seq, D_DELTA), BF16)]
    if want_state:
        out_specs.append(state_spec)
        out_shape.append(jax.ShapeDtypeStruct((b, N_DIR, N_HEADS, HEAD_DIM, HEAD_DIM), F32))
    scratch = [pltpu.VMEM((heads, seq, HEAD_DIM), F32),
               pltpu.VMEM((heads, seq, HEAD_DIM), F32),
               pltpu.VMEM((heads, seq, HEAD_DIM), F32),
               pltpu.VMEM((heads, N_DIR, seq, HEAD_DIM), F32),
               pltpu.VMEM((heads, N_DIR, seq, HEAD_DIM), BF16),
               pltpu.VMEM((heads, N_DIR, seq, HEAD_DIM), BF16),
               pltpu.VMEM((heads, N_DIR, n_chunk, HEAD_DIM, CHUNK), BF16),
               pltpu.VMEM((heads, N_DIR, n_chunk, CHUNK, CHUNK), BF16),
               pltpu.VMEM((heads, N_DIR, n_chunk, LANES), F32),
               pltpu.VMEM((heads, N_DIR, seq, HEAD_DIM), F32),
               pltpu.VMEM((heads, N_DIR, HEAD_DIM, HEAD_DIM), F32)]
    res = pl.pallas_call(
        functools.partial(_delta_kernel, seq=seq, heads=heads, unroll=unroll,
                          has_state_in=s0 is not None, has_state_out=want_state),
        grid=(b, N_HEADS // heads),
        in_specs=in_specs,
        out_specs=out_specs,
        out_shape=out_shape,
        scratch_shapes=scratch,
        compiler_params=pltpu.CompilerParams(dimension_semantics=("arbitrary", "arbitrary"),
                                             vmem_limit_bytes=VMEM_LIMIT),
    )(*args)
    return (res[0], res[1]) if want_state else (res[0], None)


def _outproj_kernel(yf_ref, yd_ref, w_ref, x_ref, gate_ref, fnw_ref, o_ref, *, final):
    y = jnp.dot(yf_ref[...], w_ref[0, :D_FOURIER, :], preferred_element_type=F32)
    y = y + jnp.dot(yd_ref[...], w_ref[0, D_FOURIER:, :], preferred_element_type=F32)
    x_new = x_ref[...] + gate_ref[0] * y
    if final:
        ms = jnp.mean(x_new * x_new, axis=-1, keepdims=True)
        x_new = x_new * lax.rsqrt(ms + EPS) * fnw_ref[...]
    o_ref[...] = x_new


def _out_projection(yf2, yd2, w_out, layer, x2, mods, mod_row, final_norm_w, seq_len, final):
    m = x2.shape[0]
    tm = 256
    assert m % tm == 0 and seq_len % tm == 0
    if mod_row is None:
        row = lambda i: (i * tm) // seq_len
    else:
        row = lambda i: mod_row
    return pl.pallas_call(
        functools.partial(_outproj_kernel, final=final),
        grid=(m // tm,),
        in_specs=[pl.BlockSpec((tm, D_FOURIER), lambda i: (i, 0)),
                  pl.BlockSpec((tm, D_DELTA), lambda i: (i, 0)),
                  pl.BlockSpec((1, D_MODEL, D_MODEL), lambda i: (layer, 0, 0)),
                  pl.BlockSpec((tm, D_MODEL), lambda i: (i, 0)),
                  pl.BlockSpec((1, 1, D_MODEL), lambda i: (row(i), 0, 2)),
                  pl.BlockSpec((1, D_MODEL), lambda i: (0, 0))],
        out_specs=pl.BlockSpec((tm, D_MODEL), lambda i: (i, 0)),
        out_shape=jax.ShapeDtypeStruct((m, D_MODEL), F32),
        compiler_params=pltpu.CompilerParams(dimension_semantics=("arbitrary",),
                                             vmem_limit_bytes=VMEM_LIMIT),
    )(yf2, yd2, w_out, x2, mods, final_norm_w)


def _mixer_layer(x2, batch, seq, mods, mod_row, norm_w, w_in, layer, wab, wabt, gate_params, conv_w,
                 gnorm_w, w_out, dft, s0, want_state, final_norm_w, final):
    alog_r, dt_r, alog_c, dt_c = gate_params
    proj, gi, gr, git = _in_projection(x2, mods, mod_row, norm_w, w_in, layer, wab, wabt,
                                       alog_r, dt_r, alog_c, dt_c, seq)
    proj3 = proj.reshape(batch, seq, D_MAIN)
    yf = _fourier_mixer(proj3, *dft)
    yd, state = _delta_mixer(proj3, conv_w, gi.reshape(batch, seq, LANES), gr.reshape(batch, seq, LANES),
                             git.reshape(LANES, batch, seq // CHUNK, LANES), gnorm_w, s0, layer, want_state,
                             heads=2 if seq > 4 * CHUNK else 4, unroll=4 if seq > 4 * CHUNK else 2)
    x_new = _out_projection(yf.reshape(batch * seq, D_FOURIER), yd.reshape(batch * seq, D_DELTA),
                            w_out, layer, x2, mods, mod_row, final_norm_w, seq, final)
    return x_new, state


def kernel(x_prompt, x_sample, state_ctx, c, c_ctx, norm_w, w_mod, b_mod, w_in, conv_w, a_log, dt_bias,
           gnorm_w, w_out, final_norm_w):
    depth = w_in.shape[0]
    bp, lp, _ = x_prompt.shape
    bs, ls, _ = x_sample.shape
    ctx_row = bs

    cvec = jnp.zeros((8, D_MODEL), F32).at[:bs].set(c).at[ctx_row].set(c_ctx)
    mods = _modulation(cvec, w_mod, b_mod)
    dft_ctx = _dft_tables(lp, False)
    dft_lat = _dft_tables(ls, True)
    fnw = final_norm_w.reshape(1, D_MODEL)
    w_in_bf = w_in.astype(BF16)
    w_gate = jnp.pad(w_in[:, :, D_MAIN:], ((0, 0), (0, 0), (0, LANES - N_GATE)))
    wab_all = w_gate.astype(BF16)
    wabt_all = jnp.swapaxes(w_gate, 1, 2).astype(BF16)
    w_out_bf = w_out.astype(BF16)

    xp = x_prompt.reshape(bp * lp, D_MODEL)
    xs = x_sample.reshape(bs * ls, D_MODEL)
    states = []
    for i in range(depth):
        wab, wabt = wab_all[i], wabt_all[i]
        pad = (0, LANES - N_DIR * N_HEADS)
        alog = jnp.pad(a_log[i].reshape(-1), pad)
        dtb = jnp.pad(dt_bias[i].reshape(-1), pad)
        gate_params = (alog.reshape(1, LANES), dtb.reshape(1, LANES),
                       alog.reshape(LANES, 1), dtb.reshape(LANES, 1))
        mods_i = mods[i].reshape(8, 1, 3 * D_MODEL)
        nw = norm_w[i].reshape(1, D_MODEL)
        gnw = gnorm_w[i].reshape(1, HEAD_DIM)
        final = i == depth - 1
        common = (nw, w_in_bf, i, wab, wabt, gate_params, conv_w[i], gnw, w_out_bf)
        xp, s_new = _mixer_layer(xp, bp, lp, mods_i, ctx_row, *common, dft_ctx, None, True, fnw, final)
        states.append(s_new)
        xs, _ = _mixer_layer(xs, bs, ls, mods_i, None, *common, dft_lat, state_ctx, False, fnw, final)
    y_prompt = xp.reshape(bp, lp, D_MODEL)
    y_sample = xs.reshape(bs, ls, D_MODEL)
    state_new = jnp.stack(states, axis=1).astype(x_prompt.dtype)
    return (y_prompt, y_sample, state_new)
```

```python
import functools
import math

import numpy as np
import jax
import jax.numpy as jnp
from jax import lax
from jax.experimental import pallas as pl
from jax.experimental.pallas import tpu as pltpu

D_MODEL = 2048
GRID_W = 64
D_FOURIER = 512
N_FOURIER_GROUPS = 4
FOURIER_GROUP = 128
D_DELTA = 1536
HEAD_DIM = 128
N_HEADS = 12
N_DIR = 2
CONV_K = 3
CHUNK = 128
EPS = 1e-6

D_MAIN = 2 * D_FOURIER + 4 * D_DELTA
N_GATE = 2 * N_DIR * N_HEADS
LANES = 128
COL_Q = (2 * D_FOURIER) // LANES
COL_K = COL_Q + N_HEADS
COL_V = COL_K + N_HEADS
COL_ZD = COL_V + N_HEADS
VMEM_LIMIT = 56 * 1024 * 1024

BF16 = jnp.bfloat16
F32 = jnp.float32
NT_DIMS = (((1,), (1,)), ((), ()))


def _mm(a, b):
    return jnp.dot(a.astype(BF16), b.astype(BF16), preferred_element_type=F32)


def _sigmoid(x):
    return 0.5 * (1.0 + jnp.tanh(0.5 * x))


def _silu(x):
    return x * _sigmoid(x)


def _mod_kernel(c_ref, w_ref, b_ref, o_ref):
    s = _silu(c_ref[...])
    o_ref[0] = _mm(s, w_ref[0]) + b_ref[0]


def _modulation(cvec, w_mod, b_mod):
    depth, _, n = w_mod.shape
    tn = 512
    return pl.pallas_call(
        _mod_kernel,
        grid=(depth, n // tn),
        in_specs=[pl.BlockSpec((8, D_MODEL), lambda l, j: (0, 0)),
                  pl.BlockSpec((1, D_MODEL, tn), lambda l, j: (l, 0, j)),
                  pl.BlockSpec((1, 1, tn), lambda l, j: (l, 0, j))],
        out_specs=pl.BlockSpec((1, 8, tn), lambda l, j: (l, 0, j)),
        out_shape=jax.ShapeDtypeStruct((depth, 8, n), F32),
        compiler_params=pltpu.CompilerParams(dimension_semantics=("arbitrary", "arbitrary")),
    )(cvec, w_mod, b_mod.reshape(depth, 1, n))


def _cast_kernel(w_ref, o_ref):
    o_ref[...] = w_ref[...].astype(BF16)


def _main_weights_bf16(w_in):
    depth = w_in.shape[0]
    tn = 1024
    return pl.pallas_call(
        _cast_kernel,
        grid=(depth, D_MAIN // tn),
        in_specs=[pl.BlockSpec((1, D_MODEL, tn), lambda l, j: (l, 0, j))],
        out_specs=pl.BlockSpec((1, D_MODEL, tn), lambda l, j: (l, 0, j)),
        out_shape=jax.ShapeDtypeStruct((depth, D_MODEL, D_MAIN), BF16),
        compiler_params=pltpu.CompilerParams(dimension_semantics=("arbitrary", "arbitrary"),
                                             vmem_limit_bytes=VMEM_LIMIT),
    )(w_in)


def _chunk_scans(lg, axis):
    n = lg.shape[axis]
    pos = lax.broadcasted_iota(jnp.int32, lg.shape, axis) % CHUNK
    pre, suf = lg, lg
    s = 1
    while s < CHUNK:
        pre = pre + jnp.where(pos >= s, pltpu.roll(pre, s, axis), 0.0)
        suf = suf + jnp.where(pos < CHUNK - s, pltpu.roll(suf, n - s, axis), 0.0)
        s *= 2
    return pre, suf


def _gate_tables(ab, a_log, dt_bias, gate_axis):
    tok_axis = 1 - gate_axis
    col = lax.broadcasted_iota(jnp.int32, ab.shape, gate_axis)
    xa = ab + dt_bias
    softplus = jnp.maximum(xa, 0.0) + jnp.log1p(jnp.exp(-jnp.abs(xa)))
    lg = jnp.where(col < N_DIR * N_HEADS, -jnp.exp(a_log) * softplus, 0.0)
    beta = _sigmoid(ab)
    pre, suf = _chunk_scans(lg, tok_axis)
    fwd = col < N_HEADS
    dec = col < N_DIR * N_HEADS
    incl = jnp.where(fwd, pre, jnp.where(dec, suf, jnp.where(col < N_GATE, beta, 0.0)))
    rest = jnp.where(fwd, suf - lg, jnp.where(dec, pre - lg, 0.0))
    return incl, rest


def _inproj_kernel(x_ref, shift_ref, scale_ref, nw_ref, w_ref, wab_ref, wabt_ref,
                   alog_r_ref, dt_r_ref, alog_c_ref, dt_c_ref,
                   proj_ref, gi_ref, gr_ref, git_ref, h_scr, *, slab):
    j = pl.program_id(1)

    @pl.when(j == 0)
    def _():
        tm = x_ref.shape[0]
        for s in range(tm // slab):
            rows = pl.ds(s * slab, slab)
            xf = x_ref[rows, :]
            ms = jnp.mean(xf * xf, axis=-1, keepdims=True)
            y = xf * lax.rsqrt(ms + EPS) * nw_ref[...]
            hb = (y * (1.0 + scale_ref[0]) + shift_ref[0]).astype(BF16)
            h_scr[rows, :] = hb
            ab = jnp.dot(hb, wab_ref[...], preferred_element_type=F32)
            gi, gr = _gate_tables(ab, alog_r_ref[...], dt_r_ref[...], 1)
            gi_ref[rows, :] = gi
            gr_ref[rows, :] = gr
            abt = lax.dot_general(wabt_ref[...], hb, NT_DIMS, preferred_element_type=F32)
            git, _ = _gate_tables(abt, alog_c_ref[...], dt_c_ref[...], 0)
            git_ref[:, s * slab:(s + 1) * slab] = git

    proj_ref[...] = jnp.dot(h_scr[...], w_ref[0], preferred_element_type=F32)


def _in_projection(x2, mods, mod_row, norm_w, w_in, layer, wab, wabt, alog_r, dt_r, alog_c, dt_c, seq_len):
    m = x2.shape[0]
    tm, tn, slab = 1024, 1024, 256
    assert m % tm == 0 and D_MAIN % tn == 0 and (seq_len % tm == 0 or tm % seq_len == 0)
    if mod_row is None:
        row = lambda i: (i * tm) // seq_len
    else:
        row = lambda i: mod_row
    const = lambda i, j: (0, 0)
    return pl.pallas_call(
        functools.partial(_inproj_kernel, slab=slab),
        grid=(m // tm, D_MAIN // tn),
        in_specs=[pl.BlockSpec((tm, D_MODEL), lambda i, j: (i, 0)),
                  pl.BlockSpec((1, 1, D_MODEL), lambda i, j: (row(i), 0, 0)),
                  pl.BlockSpec((1, 1, D_MODEL), lambda i, j: (row(i), 0, 1)),
                  pl.BlockSpec((1, D_MODEL), const),
                  pl.BlockSpec((1, D_MODEL, tn), lambda i, j: (layer, 0, j)),
                  pl.BlockSpec((D_MODEL, LANES), const),
                  pl.BlockSpec((LANES, D_MODEL), const),
                  pl.BlockSpec((1, LANES), const),
                  pl.BlockSpec((1, LANES), const),
                  pl.BlockSpec((LANES, 1), const),
                  pl.BlockSpec((LANES, 1), const)],
        out_specs=[pl.BlockSpec((tm, tn), lambda i, j: (i, j)),
                   pl.BlockSpec((tm, LANES), lambda i, j: (i, 0)),
                   pl.BlockSpec((tm, LANES), lambda i, j: (i, 0)),
                   pl.BlockSpec((LANES, tm), lambda i, j: (0, i))],
        out_shape=[jax.ShapeDtypeStruct((m, D_MAIN), F32),
                   jax.ShapeDtypeStruct((m, LANES), F32),
                   jax.ShapeDtypeStruct((m, LANES), F32),
                   jax.ShapeDtypeStruct((LANES, m), F32)],
        scratch_shapes=[pltpu.VMEM((tm, D_MODEL), BF16)],
        compiler_params=pltpu.CompilerParams(dimension_semantics=("arbitrary", "arbitrary"),
                                             vmem_limit_bytes=VMEM_LIMIT),
    )(x2, mods, mods, norm_w, w_in, wab, wabt, alog_r, dt_r, alog_c, dt_c)


def _fourier_kernel(u_ref, z_ref, cs_ref, pm_ref, o_ref, z_scr):
    t = pl.program_id(1)
    seq = u_ref.shape[1]

    @pl.when(t == 0)
    def _():
        xcs = _mm(u_ref[0], cs_ref[...])
        z_scr[0:seq, :] = xcs[:, :D_FOURIER].astype(BF16)
        z_scr[seq:2 * seq, :] = xcs[:, D_FOURIER:].astype(BF16)

    y = jnp.dot(pm_ref[...], z_scr[...], preferred_element_type=F32)
    o_ref[0] = (y * _silu(z_ref[0])).astype(BF16)


def _fourier_mixer(proj3, cs, pm):
    b, seq, _ = proj3.shape
    tl = min(seq, 512)
    return pl.pallas_call(
        _fourier_kernel,
        grid=(b, seq // tl),
        in_specs=[pl.BlockSpec((1, seq, D_FOURIER), lambda i, t: (i, 0, 0)),
                  pl.BlockSpec((1, tl, D_FOURIER), lambda i, t: (i, t, 1)),
                  pl.BlockSpec((D_FOURIER, 2 * D_FOURIER), lambda i, t: (0, 0)),
                  pl.BlockSpec((tl, 2 * seq), lambda i, t: (t, 0))],
        out_specs=pl.BlockSpec((1, tl, D_FOURIER), lambda i, t: (i, t, 0)),
        out_shape=jax.ShapeDtypeStruct((b, seq, D_FOURIER), BF16),
        scratch_shapes=[pltpu.VMEM((2 * seq, D_FOURIER), BF16)],
        compiler_params=pltpu.CompilerParams(dimension_semantics=("arbitrary", "arbitrary"),
                                             vmem_limit_bytes=VMEM_LIMIT),
    )(proj3, proj3, cs, pm)


def _dft_tables(seq, grid):
    def cs(n):
        k = np.arange(n)
        ang = 2.0 * np.pi * ((k[:, None] * k[None, :]) % n) / n
        return np.cos(ang), np.sin(ang)

    cc, sc = cs(FOURIER_GROUP)
    eye = np.eye(N_FOURIER_GROUPS)
    chan = np.concatenate([np.kron(eye, cc), np.kron(eye, sc)], axis=1) / math.sqrt(FOURIER_GROUP)
    if grid:
        cr, sr = cs(seq // GRID_W)
        cw, sw = cs(GRID_W)
        cp = np.kron(cr, cw) - np.kron(sr, sw)
        sp = np.kron(sr, cw) + np.kron(cr, sw)
    else:
        cp, sp = cs(seq)
    pos = np.concatenate([cp, -sp], axis=1) / math.sqrt(seq)
    return jnp.asarray(chan, F32).astype(BF16), jnp.asarray(pos, F32).astype(BF16)


def _delta_kernel(*refs, seq, heads, unroll, has_state_in, has_state_out):
    it = iter(refs)
    take = lambda n: [next(it) for _ in range(n)]
    q_refs, k_refs, v_refs, zd_refs = take(heads), take(heads), take(heads), take(heads)
    cwq_refs, cwk_refs, cwv_refs = take(heads), take(heads), take(heads)
    gi_ref, gr_ref = next(it), next(it)
    gt_refs = [take(N_DIR) for _ in range(heads)]
    gnw_ref = next(it)
    s0_ref = next(it) if has_state_in else None
    y_ref = next(it)
    st_ref = next(it) if has_state_out else None
    qn_s, kn_s, vv_s, u_s, w_s, qg_s, kdt_s, at_s, egl_s, o_s, s_s = take(11)

    head0 = pl.program_id(1) * heads
    n_chunk = seq // CHUNK

    row = lax.broadcasted_iota(jnp.int32, (CHUNK, HEAD_DIM), 0)

    def conv_tile(src_ref, cw_ref, r0, first, last):
        cur = src_ref[0, pl.ds(r0, CHUNK), :]
        if isinstance(r0, int):
            lo, hi = max(r0 - 8, 0), min(r0 + CHUNK, seq - 8)
        else:
            lo = pl.multiple_of(jnp.maximum(r0 - 8, 0), 8)
            hi = pl.multiple_of(jnp.minimum(r0 + CHUNK, seq - 8), 8)
        before = src_ref[0, pl.ds(lo, 8), :][7:8, :]
        after = src_ref[0, pl.ds(hi, 8), :][0:1, :]
        before = jnp.where(first, 0.0, before)
        after = jnp.where(last, 0.0, after)
        prev = jnp.where(row == 0, before, pltpu.roll(cur, 1, 0))
        nxt = jnp.where(row == CHUNK - 1, after, pltpu.roll(cur, CHUNK - 1, 0))
        cw = cw_ref[...]
        y = prev * cw[0:1, :] + cur * cw[1:2, :] + nxt * cw[2:3, :]
        return _silu(y)

    def l2n(x):
        return x * lax.rsqrt(jnp.sum(x * x, axis=-1, keepdims=True) + EPS)

    def prep(m):
        r0 = m * CHUNK if isinstance(m, int) else pl.multiple_of(m * CHUNK, CHUNK)
        first, last = m == 0, m == n_chunk - 1
        rows = pl.ds(r0, CHUNK)
        for g in range(heads):
            qn_s[g, rows, :] = l2n(conv_tile(q_refs[g], cwq_refs[g], r0, first, last)) * (HEAD_DIM ** -0.5)
            kn_s[g, rows, :] = l2n(conv_tile(k_refs[g], cwk_refs[g], r0, first, last))
            vv_s[g, rows, :] = conv_tile(v_refs[g], cwv_refs[g], r0, first, last)

    ri = lax.broadcasted_iota(jnp.int32, (CHUNK, CHUNK), 0)
    ci = lax.broadcasted_iota(jnp.int32, (CHUNK, CHUNK), 1)
    eye = (ri == ci).astype(F32)
    lane = lax.broadcasted_iota(jnp.int32, (CHUNK, LANES), 1)

    def pick(tile, c):
        return jnp.sum(jnp.where(lane == c, tile, 0.0), axis=-1, keepdims=True)

    def local(t):
        tiles = []
        for g in range(heads):
            for k in range(unroll):
                m = t * unroll + k
                rows = pl.ds(m * CHUNK if isinstance(m, int) else pl.multiple_of(m * CHUNK, CHUNK), CHUNK)
                qn, kn, vv = qn_s[g, rows, :], kn_s[g, rows, :], vv_s[g, rows, :]
                kb16 = kn.astype(BF16)
                gram = lax.dot_general(jnp.concatenate([kb16, qn.astype(BF16)], axis=0), kb16, NT_DIMS,
                                       preferred_element_type=F32)
                tiles.append((g, m, rows, qn, kn, vv, gram[:CHUNK], gram[CHUNK:]))
        chains = []
        for g, m, rows, qn, kn, vv, kk, qk in tiles:
            gi_t, gr_t = gi_ref[0, rows, :], gr_ref[0, rows, :]
            for d in range(N_DIR):
                c_g = head0 + g + d * N_HEADS
                g_col = pick(gi_t, c_g)
                b_col = pick(gi_t, c_g + N_DIR * N_HEADS)
                r_col = pick(gr_t, c_g)
                g_row = gt_refs[g][d][0, 0, pl.ds(m, 1), :]
                incl = (ri >= ci) if d == 0 else (ri <= ci)
                decay = jnp.where(incl, jnp.exp(jnp.where(incl, g_col - g_row, 0.0)), 0.0)
                lmat = jnp.where(ri == ci, 0.0, b_col * kk * decay)
                at_s[g, d, m] = (qk * decay).astype(BF16)
                e_g = jnp.exp(g_col)
                qg_s[g, d, rows, :] = (qn * e_g).astype(BF16)
                kdt_s[g, d, m] = jnp.transpose(kn * jnp.exp(r_col)).astype(BF16)
                total = g_col[0:1, :] + r_col[0:1, :]
                egl_s[g, d, pl.ds(m, 1), :] = jnp.broadcast_to(jnp.exp(total), (1, LANES))
                rhs = jnp.concatenate([vv * b_col, kn * b_col * e_g], axis=1).astype(BF16)
                chains.append((g, d, rows, lmat, rhs))
        tinvs = [eye - lmat for _, _, _, lmat, _ in chains]
        powers = [lmat.astype(BF16) for _, _, _, lmat, _ in chains]
        powers = [jnp.dot(p, p, preferred_element_type=F32).astype(BF16) for p in powers]
        for _ in range(5):
            prods = [jnp.dot(jnp.concatenate([ti.astype(BF16), p], axis=0), p, preferred_element_type=F32)
                     for ti, p in zip(tinvs, powers)]
            tinvs = [ti + pr[:CHUNK] for ti, pr in zip(tinvs, prods)]
            powers = [pr[CHUNK:].astype(BF16) for pr in prods]
        tinvs = [ti + jnp.dot(ti.astype(BF16), p, preferred_element_type=F32) for ti, p in zip(tinvs, powers)]
        uws = [jnp.dot(ti.astype(BF16), rhs, preferred_element_type=F32)
               for ti, (_, _, _, _, rhs) in zip(tinvs, chains)]
        for uw, (g, d, rows, _, _) in zip(uws, chains):
            u_s[g, d, rows, :] = uw[:, :HEAD_DIM]
            w_s[g, d, rows, :] = uw[:, HEAD_DIM:].astype(BF16)

    n_iter = n_chunk // unroll
    for k in range(unroll):
        prep(k)

    def local_and_next_prep(t, carry):
        for k in range(unroll):
            prep((t + 1) * unroll + k)
        local(t)
        return carry

    lax.fori_loop(0, n_iter - 1, local_and_next_prep, 0)
    local(n_iter - 1)

    for g in range(heads):
        for d in range(N_DIR):
            s_s[g, d] = s0_ref[0, 0, d, g] if has_state_in else jnp.zeros((HEAD_DIM, HEAD_DIM), F32)

    def step(n):
        chains = [(g, d) for g in range(heads) for d in range(N_DIR)]
        where = []
        for g, d in chains:
            c = n if d == 0 else n_chunk - 1 - n
            where.append((c, pl.ds(pl.multiple_of(c * CHUNK, CHUNK), CHUNK)))
        s_old = [s_s[g, d] for g, d in chains]
        sws = [jnp.dot(jnp.concatenate([w_s[g, d, rows, :], qg_s[g, d, rows, :]], axis=0),
                       so.astype(BF16), preferred_element_type=F32)
               for (g, d), (c, rows), so in zip(chains, where, s_old)]
        vns = [(u_s[g, d, rows, :] - sw[:CHUNK]).astype(BF16)
               for (g, d), (c, rows), sw in zip(chains, where, sws)]
        outs = [jnp.dot(jnp.concatenate([at_s[g, d, c], kdt_s[g, d, c]], axis=0), vn,
                        preferred_element_type=F32)
                for (g, d), (c, rows), vn in zip(chains, where, vns)]
        for (g, d), (c, rows), so, sw, out in zip(chains, where, s_old, sws, outs):
            o_s[g, d, rows, :] = sw[CHUNK:] + out[:CHUNK]
            s_s[g, d] = so * egl_s[g, d, pl.ds(c, 1), :] + out[CHUNK:]

    def finish(c):
        rows = pl.ds(c * CHUNK if isinstance(c, int) else pl.multiple_of(c * CHUNK, CHUNK), CHUNK)
        for g in range(heads):
            o = o_s[g, 0, rows, :] + o_s[g, 1, rows, :]
            y = o * lax.rsqrt(jnp.mean(o * o, axis=-1, keepdims=True) + EPS) * gnw_ref[...]
            y_ref[0, rows, g * HEAD_DIM:(g + 1) * HEAD_DIM] = (y * _silu(zd_refs[g][0, rows, :])).astype(BF16)

    half = n_chunk // 2 + 1

    def plain_step(n, carry):
        step(n)
        return carry

    def step_and_finish(n, carry):
        finish(n - 1)
        finish(n_chunk - n)
        step(n)
        return carry

    lax.fori_loop(0, half, plain_step, 0)
    lax.fori_loop(half, n_chunk, step_and_finish, 0)
    finish(n_chunk - 1)
    finish(0)

    if has_state_out:
        for g in range(heads):
            for d in range(N_DIR):
                st_ref[0, d, g] = s_s[g, d]


def _delta_mixer(proj3, conv_w, gi3, gr3, git4, gnorm_w, s0, layer, want_state, heads, unroll):
    b, seq, _ = proj3.shape
    n_chunk = seq // CHUNK
    assert N_HEADS % heads == 0 and n_chunk % unroll == 0
    slots = range(heads)
    tok = lambda col0: [pl.BlockSpec((1, seq, HEAD_DIM), lambda i, h, g=g: (i, 0, col0 + h * heads + g))
                        for g in slots]
    cw = lambda col0: [pl.BlockSpec((CONV_K, HEAD_DIM), lambda i, h, g=g: (0, col0 + h * heads + g))
                       for g in slots]
    gate_rows = [pl.BlockSpec((1, 1, n_chunk, LANES), lambda i, h, g=g, d=d: (d * N_HEADS + h * heads + g, i, 0, 0))
                 for g in slots for d in range(N_DIR)]
    in_specs = (tok(COL_Q) + tok(COL_K) + tok(COL_V) + tok(COL_ZD) + cw(0) + cw(N_HEADS) + cw(2 * N_HEADS)
                + [pl.BlockSpec((1, seq, LANES), lambda i, h: (i, 0, 0)),
                   pl.BlockSpec((1, seq, LANES), lambda i, h: (i, 0, 0))]
                + gate_rows + [pl.BlockSpec((1, HEAD_DIM), lambda i, h: (0, 0))])
    args = [proj3] * (4 * heads) + [conv_w] * (3 * heads) + [gi3, gr3] + [git4] * (N_DIR * heads) + [gnorm_w]
    state_spec = pl.BlockSpec((1, N_DIR, heads, HEAD_DIM, HEAD_DIM), lambda i, h: (i, 0, h, 0, 0))
    if s0 is not None:
        in_specs.append(pl.BlockSpec((1, 1, N_DIR, heads, HEAD_DIM, HEAD_DIM),
                                     lambda i, h: (i, layer, 0, h, 0, 0)))
        args.append(s0)
    out_specs = [pl.BlockSpec((1, seq, heads * HEAD_DIM), lambda i, h: (i, 0, h))]
    out_shape = [jax.ShapeDtypeStruct((b, seq, D_DELTA), BF16)]
    if want_state:
        out_specs.append(state_spec)
        out_shape.append(jax.ShapeDtypeStruct((b, N_DIR, N_HEADS, HEAD_DIM, HEAD_DIM), F32))
    scratch = [pltpu.VMEM((heads, seq, HEAD_DIM), F32),
               pltpu.VMEM((heads, seq, HEAD_DIM), F32),
               pltpu.VMEM((heads, seq, HEAD_DIM), F32),
               pltpu.VMEM((heads, N_DIR, seq, HEAD_DIM), F32),
               pltpu.VMEM((heads, N_DIR, seq, HEAD_DIM), BF16),
               pltpu.VMEM((heads, N_DIR, seq, HEAD_DIM), BF16),
               pltpu.VMEM((heads, N_DIR, n_chunk, HEAD_DIM, CHUNK), BF16),
               pltpu.VMEM((heads, N_DIR, n_chunk, CHUNK, CHUNK), BF16),
               pltpu.VMEM((heads, N_DIR, n_chunk, LANES), F32),
               pltpu.VMEM((heads, N_DIR, seq, HEAD_DIM), F32),
               pltpu.VMEM((heads, N_DIR, HEAD_DIM, HEAD_DIM), F32)]
    res = pl.pallas_call(
        functools.partial(_delta_kernel, seq=seq, heads=heads, unroll=unroll,
                          has_state_in=s0 is not None, has_state_out=want_state),
        grid=(b, N_HEADS // heads),
        in_specs=in_specs,
        out_specs=out_specs,
        out_shape=out_shape,
        scratch_shapes=scratch,
        compiler_params=pltpu.CompilerParams(dimension_semantics=("arbitrary", "arbitrary"),
                                             vmem_limit_bytes=VMEM_LIMIT),
    )(*args)
    return (res[0], res[1]) if want_state else (res[0], None)


def _outproj_kernel(yf_ref, yd_ref, w_ref, x_ref, gate_ref, fnw_ref, o_ref, *, final):
    y = jnp.dot(yf_ref[...], w_ref[0, :D_FOURIER, :], preferred_element_type=F32)
    y = y + jnp.dot(yd_ref[...], w_ref[0, D_FOURIER:, :], preferred_element_type=F32)
    x_new = x_ref[...] + gate_ref[0] * y
    if final:
        ms = jnp.mean(x_new * x_new, axis=-1, keepdims=True)
        x_new = x_new * lax.rsqrt(ms + EPS) * fnw_ref[...]
    o_ref[...] = x_new


def _out_projection(yf2, yd2, w_out, layer, x2, mods, mod_row, final_norm_w, seq_len, final):
    m = x2.shape[0]
    tm = min(512, seq_len)
    assert m % tm == 0 and seq_len % tm == 0
    if mod_row is None:
        row = lambda i: (i * tm) // seq_len
    else:
        row = lambda i: mod_row
    return pl.pallas_call(
        functools.partial(_outproj_kernel, final=final),
        grid=(m // tm,),
        in_specs=[pl.BlockSpec((tm, D_FOURIER), lambda i: (i, 0)),
                  pl.BlockSpec((tm, D_DELTA), lambda i: (i, 0)),
                  pl.BlockSpec((1, D_MODEL, D_MODEL), lambda i: (layer, 0, 0)),
                  pl.BlockSpec((tm, D_MODEL), lambda i: (i, 0)),
                  pl.BlockSpec((1, 1, D_MODEL), lambda i: (row(i), 0, 2)),
                  pl.BlockSpec((1, D_MODEL), lambda i: (0, 0))],
        out_specs=pl.BlockSpec((tm, D_MODEL), lambda i: (i, 0)),
        out_shape=jax.ShapeDtypeStruct((m, D_MODEL), F32),
        compiler_params=pltpu.CompilerParams(dimension_semantics=("arbitrary",),
                                             vmem_limit_bytes=VMEM_LIMIT),
    )(yf2, yd2, w_out, x2, mods, final_norm_w)


def _mixer_layer(x2, batch, seq, mods, mod_row, norm_w, w_in, layer, wab, wabt, gate_params, conv_w,
                 gnorm_w, w_out, dft, s0, want_state, final_norm_w, final):
    alog_r, dt_r, alog_c, dt_c = gate_params
    proj, gi, gr, git = _in_projection(x2, mods, mod_row, norm_w, w_in, layer, wab, wabt,
                                       alog_r, dt_r, alog_c, dt_c, seq)
    proj3 = proj.reshape(batch, seq, D_MAIN)
    yf = _fourier_mixer(proj3, *dft)
    yd, state = _delta_mixer(proj3, conv_w, gi.reshape(batch, seq, LANES), gr.reshape(batch, seq, LANES),
                             git.reshape(LANES, batch, seq // CHUNK, LANES), gnorm_w, s0, layer, want_state,
                             heads=2 if seq > 4 * CHUNK else 4, unroll=4 if seq > 4 * CHUNK else 2)
    x_new = _out_projection(yf.reshape(batch * seq, D_FOURIER), yd.reshape(batch * seq, D_DELTA),
                            w_out, layer, x2, mods, mod_row, final_norm_w, seq, final)
    return x_new, state


def kernel(x_prompt, x_sample, state_ctx, c, c_ctx, norm_w, w_mod, b_mod, w_in, conv_w, a_log, dt_bias,
           gnorm_w, w_out, final_norm_w):
    depth = w_in.shape[0]
    bp, lp, _ = x_prompt.shape
    bs, ls, _ = x_sample.shape
    ctx_row = bs

    cvec = jnp.zeros((8, D_MODEL), F32).at[:bs].set(c).at[ctx_row].set(c_ctx)
    mods = _modulation(cvec, w_mod, b_mod)
    dft_ctx = _dft_tables(lp, False)
    dft_lat = _dft_tables(ls, True)
    fnw = final_norm_w.reshape(1, D_MODEL)
    w_in_bf = _main_weights_bf16(w_in)
    w_gate = jnp.pad(w_in[:, :, D_MAIN:], ((0, 0), (0, 0), (0, LANES - N_GATE)))
    wab_all = w_gate.astype(BF16)
    wabt_all = jnp.swapaxes(w_gate, 1, 2).astype(BF16)
    w_out_bf = w_out.astype(BF16)

    xp = x_prompt.reshape(bp * lp, D_MODEL)
    xs = x_sample.reshape(bs * ls, D_MODEL)
    states = []
    for i in range(depth):
        wab, wabt = wab_all[i], wabt_all[i]
        pad = (0, LANES - N_DIR * N_HEADS)
        alog = jnp.pad(a_log[i].reshape(-1), pad)
        dtb = jnp.pad(dt_bias[i].reshape(-1), pad)
        gate_params = (alog.reshape(1, LANES), dtb.reshape(1, LANES),
                       alog.reshape(LANES, 1), dtb.reshape(LANES, 1))
        mods_i = mods[i].reshape(8, 1, 3 * D_MODEL)
        nw = norm_w[i].reshape(1, D_MODEL)
        gnw = gnorm_w[i].reshape(1, HEAD_DIM)
        final = i == depth - 1
        common = (nw, w_in_bf, i, wab, wabt, gate_params, conv_w[i], gnw, w_out_bf)
        xp, s_new = _mixer_layer(xp, bp, lp, mods_i, ctx_row, *common, dft_ctx, None, True, fnw, final)
        states.append(s_new)
        xs, _ = _mixer_layer(xs, bs, ls, mods_i, None, *common, dft_lat, state_ctx, False, fnw, final)
    y_prompt = xp.reshape(bp, lp, D_MODEL)
    y_sample = xs.reshape(bs, ls, D_MODEL)
    state_new = jnp.stack(states, axis=1).astype(x_prompt.dtype)
    return (y_prompt, y_sample, state_new)
```

```python
import functools
import math

import numpy as np
import jax
import jax.numpy as jnp
from jax import lax
from jax.experimental import pallas as pl
from jax.experimental.pallas import tpu as pltpu

D_MODEL = 2048
GRID_W = 64
D_FOURIER = 512
N_FOURIER_GROUPS = 4
FOURIER_GROUP = 128
D_DELTA = 1536
HEAD_DIM = 128
N_HEADS = 12
N_DIR = 2
CONV_K = 3
CHUNK = 128
EPS = 1e-6

D_MAIN = 2 * D_FOURIER + 4 * D_DELTA
N_GATE = 2 * N_DIR * N_HEADS
LANES = 128
COL_Q = (2 * D_FOURIER) // LANES
COL_K = COL_Q + N_HEADS
COL_V = COL_K + N_HEADS
COL_ZD = COL_V + N_HEADS
VMEM_LIMIT = 56 * 1024 * 1024

BF16 = jnp.bfloat16
F32 = jnp.float32
NT_DIMS = (((1,), (1,)), ((), ()))


def _mm(a, b):
    return jnp.dot(a.astype(BF16), b.astype(BF16), preferred_element_type=F32)


def _sigmoid(x):
    return 0.5 * (1.0 + jnp.tanh(0.5 * x))


def _silu(x):
    return x * _sigmoid(x)


def _mod_kernel(c_ref, w_ref, b_ref, o_ref):
    s = _silu(c_ref[...])
    o_ref[0] = _mm(s, w_ref[0]) + b_ref[0]


def _modulation(cvec, w_mod, b_mod):
    depth, _, n = w_mod.shape
    tn = 512
    return pl.pallas_call(
        _mod_kernel,
        grid=(depth, n // tn),
        in_specs=[pl.BlockSpec((8, D_MODEL), lambda l, j: (0, 0)),
                  pl.BlockSpec((1, D_MODEL, tn), lambda l, j: (l, 0, j)),
                  pl.BlockSpec((1, 1, tn), lambda l, j: (l, 0, j))],
        out_specs=pl.BlockSpec((1, 8, tn), lambda l, j: (l, 0, j)),
        out_shape=jax.ShapeDtypeStruct((depth, 8, n), F32),
        compiler_params=pltpu.CompilerParams(dimension_semantics=("arbitrary", "arbitrary")),
    )(cvec, w_mod, b_mod.reshape(depth, 1, n))


def _chunk_scans(lg, axis):
    n = lg.shape[axis]
    pos = lax.broadcasted_iota(jnp.int32, lg.shape, axis) % CHUNK
    pre, suf = lg, lg
    s = 1
    while s < CHUNK:
        pre = pre + jnp.where(pos >= s, pltpu.roll(pre, s, axis), 0.0)
        suf = suf + jnp.where(pos < CHUNK - s, pltpu.roll(suf, n - s, axis), 0.0)
        s *= 2
    return pre, suf


def _gate_tables(ab, a_log, dt_bias, gate_axis):
    tok_axis = 1 - gate_axis
    col = lax.broadcasted_iota(jnp.int32, ab.shape, gate_axis)
    xa = ab + dt_bias
    softplus = jnp.maximum(xa, 0.0) + jnp.log1p(jnp.exp(-jnp.abs(xa)))
    lg = jnp.where(col < N_DIR * N_HEADS, -jnp.exp(a_log) * softplus, 0.0)
    beta = _sigmoid(ab)
    pre, suf = _chunk_scans(lg, tok_axis)
    fwd = col < N_HEADS
    dec = col < N_DIR * N_HEADS
    incl = jnp.where(fwd, pre, jnp.where(dec, suf, jnp.where(col < N_GATE, beta, 0.0)))
    rest = jnp.where(fwd, suf - lg, jnp.where(dec, pre - lg, 0.0))
    return incl, rest


def _inproj_kernel(x_ref, shift_ref, scale_ref, nw_ref, w_ref, wab_ref, wabt_ref,
                   alog_r_ref, dt_r_ref, alog_c_ref, dt_c_ref,
                   proj_ref, gi_ref, gr_ref, git_ref, h_scr, *, slab):
    j = pl.program_id(1)

    @pl.when(j == 0)
    def _():
        tm = x_ref.shape[0]
        for s in range(tm // slab):
            rows = pl.ds(s * slab, slab)
            xf = x_ref[rows, :]
            ms = jnp.mean(xf * xf, axis=-1, keepdims=True)
            y = xf * lax.rsqrt(ms + EPS) * nw_ref[...]
            hb = (y * (1.0 + scale_ref[0]) + shift_ref[0]).astype(BF16)
            h_scr[rows, :] = hb
            ab = jnp.dot(hb, wab_ref[...], preferred_element_type=F32)
            gi, gr = _gate_tables(ab, alog_r_ref[...], dt_r_ref[...], 1)
            gi_ref[rows, :] = gi
            gr_ref[rows, :] = gr
            abt = lax.dot_general(wabt_ref[...], hb, NT_DIMS, preferred_element_type=F32)
            git, _ = _gate_tables(abt, alog_c_ref[...], dt_c_ref[...], 0)
            git_ref[:, s * slab:(s + 1) * slab] = git

    proj_ref[...] = lax.dot_general(h_scr[...], w_ref[0], NT_DIMS, preferred_element_type=F32)


def _in_projection(x2, mods, mod_row, norm_w, w_in, layer, wab, wabt, alog_r, dt_r, alog_c, dt_c, seq_len):
    m = x2.shape[0]
    tm, tn, slab = 1024, 1024, 256
    assert m % tm == 0 and D_MAIN % tn == 0 and (seq_len % tm == 0 or tm % seq_len == 0)
    if mod_row is None:
        row = lambda i: (i * tm) // seq_len
    else:
        row = lambda i: mod_row
    const = lambda i, j: (0, 0)
    return pl.pallas_call(
        functools.partial(_inproj_kernel, slab=slab),
        grid=(m // tm, D_MAIN // tn),
        in_specs=[pl.BlockSpec((tm, D_MODEL), lambda i, j: (i, 0)),
                  pl.BlockSpec((1, 1, D_MODEL), lambda i, j: (row(i), 0, 0)),
                  pl.BlockSpec((1, 1, D_MODEL), lambda i, j: (row(i), 0, 1)),
                  pl.BlockSpec((1, D_MODEL), const),
                  pl.BlockSpec((1, tn, D_MODEL), lambda i, j: (layer, j, 0)),
                  pl.BlockSpec((D_MODEL, LANES), const),
                  pl.BlockSpec((LANES, D_MODEL), const),
                  pl.BlockSpec((1, LANES), const),
                  pl.BlockSpec((1, LANES), const),
                  pl.BlockSpec((LANES, 1), const),
                  pl.BlockSpec((LANES, 1), const)],
        out_specs=[pl.BlockSpec((tm, tn), lambda i, j: (i, j)),
                   pl.BlockSpec((tm, LANES), lambda i, j: (i, 0)),
                   pl.BlockSpec((tm, LANES), lambda i, j: (i, 0)),
                   pl.BlockSpec((LANES, tm), lambda i, j: (0, i))],
        out_shape=[jax.ShapeDtypeStruct((m, D_MAIN), F32),
                   jax.ShapeDtypeStruct((m, LANES), F32),
                   jax.ShapeDtypeStruct((m, LANES), F32),
                   jax.ShapeDtypeStruct((LANES, m), F32)],
        scratch_shapes=[pltpu.VMEM((tm, D_MODEL), BF16)],
        compiler_params=pltpu.CompilerParams(dimension_semantics=("arbitrary", "arbitrary"),
                                             vmem_limit_bytes=VMEM_LIMIT),
    )(x2, mods, mods, norm_w, w_in, wab, wabt, alog_r, dt_r, alog_c, dt_c)


def _fourier_kernel(u_ref, z_ref, cs_ref, pm_ref, o_ref, z_scr):
    t = pl.program_id(1)
    seq = u_ref.shape[1]

    @pl.when(t == 0)
    def _():
        xcs = _mm(u_ref[0], cs_ref[...])
        z_scr[0:seq, :] = xcs[:, :D_FOURIER].astype(BF16)
        z_scr[seq:2 * seq, :] = xcs[:, D_FOURIER:].astype(BF16)

    y = jnp.dot(pm_ref[...], z_scr[...], preferred_element_type=F32)
    o_ref[0] = (y * _silu(z_ref[0])).astype(BF16)


def _fourier_mixer(proj3, cs, pm):
    b, seq, _ = proj3.shape
    tl = min(seq, 512)
    return pl.pallas_call(
        _fourier_kernel,
        grid=(b, seq // tl),
        in_specs=[pl.BlockSpec((1, seq, D_FOURIER), lambda i, t: (i, 0, 0)),
                  pl.BlockSpec((1, tl, D_FOURIER), lambda i, t: (i, t, 1)),
                  pl.BlockSpec((D_FOURIER, 2 * D_FOURIER), lambda i, t: (0, 0)),
                  pl.BlockSpec((tl, 2 * seq), lambda i, t: (t, 0))],
        out_specs=pl.BlockSpec((1, tl, D_FOURIER), lambda i, t: (i, t, 0)),
        out_shape=jax.ShapeDtypeStruct((b, seq, D_FOURIER), BF16),
        scratch_shapes=[pltpu.VMEM((2 * seq, D_FOURIER), BF16)],
        compiler_params=pltpu.CompilerParams(dimension_semantics=("arbitrary", "arbitrary"),
                                             vmem_limit_bytes=VMEM_LIMIT),
    )(proj3, proj3, cs, pm)


def _dft_tables(seq, grid):
    def cs(n):
        k = np.arange(n)
        ang = 2.0 * np.pi * ((k[:, None] * k[None, :]) % n) / n
        return np.cos(ang), np.sin(ang)

    cc, sc = cs(FOURIER_GROUP)
    eye = np.eye(N_FOURIER_GROUPS)
    chan = np.concatenate([np.kron(eye, cc), np.kron(eye, sc)], axis=1) / math.sqrt(FOURIER_GROUP)
    if grid:
        cr, sr = cs(seq // GRID_W)
        cw, sw = cs(GRID_W)
        cp = np.kron(cr, cw) - np.kron(sr, sw)
        sp = np.kron(sr, cw) + np.kron(cr, sw)
    else:
        cp, sp = cs(seq)
    pos = np.concatenate([cp, -sp], axis=1) / math.sqrt(seq)
    return jnp.asarray(chan, F32).astype(BF16), jnp.asarray(pos, F32).astype(BF16)


def _delta_kernel(*refs, seq, heads, unroll, has_state_in, has_state_out):
    it = iter(refs)
    take = lambda n: [next(it) for _ in range(n)]
    q_refs, k_refs, v_refs, zd_refs = take(heads), take(heads), take(heads), take(heads)
    cwq_refs, cwk_refs, cwv_refs = take(heads), take(heads), take(heads)
    gi_ref, gr_ref = next(it), next(it)
    gt_refs = [take(N_DIR) for _ in range(heads)]
    gnw_ref = next(it)
    s0_ref = next(it) if has_state_in else None
    y_ref = next(it)
    st_ref = next(it) if has_state_out else None
    qn_s, kn_s, vv_s, u_s, w_s, qg_s, kdt_s, at_s, egl_s, o_s, s_s = take(11)

    head0 = pl.program_id(1) * heads
    n_chunk = seq // CHUNK

    row = lax.broadcasted_iota(jnp.int32, (CHUNK, HEAD_DIM), 0)

    def conv_tile(src_ref, cw_ref, r0, first, last):
        cur = src_ref[0, pl.ds(r0, CHUNK), :]
        if isinstance(r0, int):
            lo, hi = max(r0 - 8, 0), min(r0 + CHUNK, seq - 8)
        else:
            lo = pl.multiple_of(jnp.maximum(r0 - 8, 0), 8)
            hi = pl.multiple_of(jnp.minimum(r0 + CHUNK, seq - 8), 8)
        before = src_ref[0, pl.ds(lo, 8), :][7:8, :]
        after = src_ref[0, pl.ds(hi, 8), :][0:1, :]
        before = jnp.where(first, 0.0, before)
        after = jnp.where(last, 0.0, after)
        prev = jnp.where(row == 0, before, pltpu.roll(cur, 1, 0))
        nxt = jnp.where(row == CHUNK - 1, after, pltpu.roll(cur, CHUNK - 1, 0))
        cw = cw_ref[...]
        y = prev * cw[0:1, :] + cur * cw[1:2, :] + nxt * cw[2:3, :]
        return _silu(y)

    def l2n(x):
        return x * lax.rsqrt(jnp.sum(x * x, axis=-1, keepdims=True) + EPS)

    def prep(m):
        r0 = m * CHUNK if isinstance(m, int) else pl.multiple_of(m * CHUNK, CHUNK)
        first, last = m == 0, m == n_chunk - 1
        rows = pl.ds(r0, CHUNK)
        for g in range(heads):
            qn_s[g, rows, :] = l2n(conv_tile(q_refs[g], cwq_refs[g], r0, first, last)) * (HEAD_DIM ** -0.5)
            kn_s[g, rows, :] = l2n(conv_tile(k_refs[g], cwk_refs[g], r0, first, last))
            vv_s[g, rows, :] = conv_tile(v_refs[g], cwv_refs[g], r0, first, last)

    ri = lax.broadcasted_iota(jnp.int32, (CHUNK, CHUNK), 0)
    ci = lax.broadcasted_iota(jnp.int32, (CHUNK, CHUNK), 1)
    eye = (ri == ci).astype(F32)
    lane = lax.broadcasted_iota(jnp.int32, (CHUNK, LANES), 1)

    def pick(tile, c):
        return jnp.sum(jnp.where(lane == c, tile, 0.0), axis=-1, keepdims=True)

    def local(t):
        tiles = []
        for g in range(heads):
            for k in range(unroll):
                m = t * unroll + k
                rows = pl.ds(m * CHUNK if isinstance(m, int) else pl.multiple_of(m * CHUNK, CHUNK), CHUNK)
                qn, kn, vv = qn_s[g, rows, :], kn_s[g, rows, :], vv_s[g, rows, :]
                kb16 = kn.astype(BF16)
                gram = lax.dot_general(jnp.concatenate([kb16, qn.astype(BF16)], axis=0), kb16, NT_DIMS,
                                       preferred_element_type=F32)
                tiles.append((g, m, rows, qn, kn, vv, gram[:CHUNK], gram[CHUNK:]))
        chains = []
        for g, m, rows, qn, kn, vv, kk, qk in tiles:
            gi_t, gr_t = gi_ref[0, rows, :], gr_ref[0, rows, :]
            for d in range(N_DIR):
                c_g = head0 + g + d * N_HEADS
                g_col = pick(gi_t, c_g)
                b_col = pick(gi_t, c_g + N_DIR * N_HEADS)
                r_col = pick(gr_t, c_g)
                g_row = gt_refs[g][d][0, 0, pl.ds(m, 1), :]
                incl = (ri >= ci) if d == 0 else (ri <= ci)
                decay = jnp.where(incl, jnp.exp(jnp.where(incl, g_col - g_row, 0.0)), 0.0)
                lmat = jnp.where(ri == ci, 0.0, b_col * kk * decay)
                at_s[g, d, m] = (qk * decay).astype(BF16)
                e_g = jnp.exp(g_col)
                qg_s[g, d, rows, :] = (qn * e_g).astype(BF16)
                kdt_s[g, d, m] = jnp.transpose(kn * jnp.exp(r_col)).astype(BF16)
                total = g_col[0:1, :] + r_col[0:1, :]
                egl_s[g, d, pl.ds(m, 1), :] = jnp.broadcast_to(jnp.exp(total), (1, LANES))
                rhs = jnp.concatenate([vv * b_col, kn * b_col * e_g], axis=1).astype(BF16)
                chains.append((g, d, rows, lmat, rhs))
        tinvs = [eye - lmat for _, _, _, lmat, _ in chains]
        powers = [lmat.astype(BF16) for _, _, _, lmat, _ in chains]
        powers = [jnp.dot(p, p, preferred_element_type=F32).astype(BF16) for p in powers]
        for _ in range(5):
            prods = [jnp.dot(jnp.concatenate([ti.astype(BF16), p], axis=0), p, preferred_element_type=F32)
                     for ti, p in zip(tinvs, powers)]
            tinvs = [ti + pr[:CHUNK] for ti, pr in zip(tinvs, prods)]
            powers = [pr[CHUNK:].astype(BF16) for pr in prods]
        tinvs = [ti + jnp.dot(ti.astype(BF16), p, preferred_element_type=F32) for ti, p in zip(tinvs, powers)]
        uws = [jnp.dot(ti.astype(BF16), rhs, preferred_element_type=F32)
               for ti, (_, _, _, _, rhs) in zip(tinvs, chains)]
        for uw, (g, d, rows, _, _) in zip(uws, chains):
            u_s[g, d, rows, :] = uw[:, :HEAD_DIM]
            w_s[g, d, rows, :] = uw[:, HEAD_DIM:].astype(BF16)

    n_iter = n_chunk // unroll
    for k in range(unroll):
        prep(k)

    def local_and_next_prep(t, carry):
        for k in range(unroll):
            prep((t + 1) * unroll + k)
        local(t)
        return carry

    lax.fori_loop(0, n_iter - 1, local_and_next_prep, 0)
    local(n_iter - 1)

    for g in range(heads):
        for d in range(N_DIR):
            s_s[g, d] = s0_ref[0, 0, d, g] if has_state_in else jnp.zeros((HEAD_DIM, HEAD_DIM), F32)

    def step(n):
        chains = [(g, d) for g in range(heads) for d in range(N_DIR)]
        where = []
        for g, d in chains:
            c = n if d == 0 else n_chunk - 1 - n
            where.append((c, pl.ds(pl.multiple_of(c * CHUNK, CHUNK), CHUNK)))
        s_old = [s_s[g, d] for g, d in chains]
        sws = [jnp.dot(jnp.concatenate([w_s[g, d, rows, :], qg_s[g, d, rows, :]], axis=0),
                       so.astype(BF16), preferred_element_type=F32)
               for (g, d), (c, rows), so in zip(chains, where, s_old)]
        vns = [(u_s[g, d, rows, :] - sw[:CHUNK]).astype(BF16)
               for (g, d), (c, rows), sw in zip(chains, where, sws)]
        outs = [jnp.dot(jnp.concatenate([at_s[g, d, c], kdt_s[g, d, c]], axis=0), vn,
                        preferred_element_type=F32)
                for (g, d), (c, rows), vn in zip(chains, where, vns)]
        for (g, d), (c, rows), so, sw, out in zip(chains, where, s_old, sws, outs):
            o_s[g, d, rows, :] = sw[CHUNK:] + out[:CHUNK]
            s_s[g, d] = so * egl_s[g, d, pl.ds(c, 1), :] + out[CHUNK:]

    def finish(c):
        rows = pl.ds(c * CHUNK if isinstance(c, int) else pl.multiple_of(c * CHUNK, CHUNK), CHUNK)
        for g in range(heads):
            o = o_s[g, 0, rows, :] + o_s[g, 1, rows, :]
            y = o * lax.rsqrt(jnp.mean(o * o, axis=-1, keepdims=True) + EPS) * gnw_ref[...]
            y_ref[0, rows, g * HEAD_DIM:(g + 1) * HEAD_DIM] = (y * _silu(zd_refs[g][0, rows, :])).astype(BF16)

    half = n_chunk // 2 + 1

    def plain_step(n, carry):
        step(n)
        return carry

    def step_and_finish(n, carry):
        finish(n - 1)
        finish(n_chunk - n)
        step(n)
        return carry

    lax.fori_loop(0, half, plain_step, 0)
    lax.fori_loop(half, n_chunk, step_and_finish, 0)
    finish(n_chunk - 1)
    finish(0)

    if has_state_out:
        for g in range(heads):
            for d in range(N_DIR):
                st_ref[0, d, g] = s_s[g, d]


def _delta_mixer(proj3, conv_w, gi3, gr3, git4, gnorm_w, s0, layer, want_state, heads, unroll):
    b, seq, _ = proj3.shape
    n_chunk = seq // CHUNK
    assert N_HEADS % heads == 0 and n_chunk % unroll == 0
    slots = range(heads)
    tok = lambda col0: [pl.BlockSpec((1, seq, HEAD_DIM), lambda i, h, g=g: (i, 0, col0 + h * heads + g))
                        for g in slots]
    cw = lambda col0: [pl.BlockSpec((CONV_K, HEAD_DIM), lambda i, h, g=g: (0, col0 + h * heads + g))
                       for g in slots]
    gate_rows = [pl.BlockSpec((1, 1, n_chunk, LANES), lambda i, h, g=g, d=d: (d * N_HEADS + h * heads + g, i, 0, 0))
                 for g in slots for d in range(N_DIR)]
    in_specs = (tok(COL_Q) + tok(COL_K) + tok(COL_V) + tok(COL_ZD) + cw(0) + cw(N_HEADS) + cw(2 * N_HEADS)
                + [pl.BlockSpec((1, seq, LANES), lambda i, h: (i, 0, 0)),
                   pl.BlockSpec((1, seq, LANES), lambda i, h: (i, 0, 0))]
                + gate_rows + [pl.BlockSpec((1, HEAD_DIM), lambda i, h: (0, 0))])
    args = [proj3] * (4 * heads) + [conv_w] * (3 * heads) + [gi3, gr3] + [git4] * (N_DIR * heads) + [gnorm_w]
    state_spec = pl.BlockSpec((1, N_DIR, heads, HEAD_DIM, HEAD_DIM), lambda i, h: (i, 0, h, 0, 0))
    if s0 is not None:
        in_specs.append(pl.BlockSpec((1, 1, N_DIR, heads, HEAD_DIM, HEAD_DIM),
                                     lambda i, h: (i, layer, 0, h, 0, 0)))
        args.append(s0)
    out_specs = [pl.BlockSpec((1, seq, heads * HEAD_DIM), lambda i, h: (i, 0, h))]
    out_shape = [jax.ShapeDtypeStruct((b, seq, D_DELTA), BF16)]
    if want_state:
        out_specs.append(state_spec)
        out_shape.append(jax.ShapeDtypeStruct((b, N_DIR, N_HEADS, HEAD_DIM, HEAD_DIM), F32))
    scratch = [pltpu.VMEM((heads, seq, HEAD_DIM), F32),
               pltpu.VMEM((heads, seq, HEAD_DIM), F32),
               pltpu.VMEM((heads, seq, HEAD_DIM), F32),
               pltpu.VMEM((heads, N_DIR, seq, HEAD_DIM), F32),
               pltpu.VMEM((heads, N_DIR, seq, HEAD_DIM), BF16),
               pltpu.VMEM((heads, N_DIR, seq, HEAD_DIM), BF16),
               pltpu.VMEM((heads, N_DIR, n_chunk, HEAD_DIM, CHUNK), BF16),
               pltpu.VMEM((heads, N_DIR, n_chunk, CHUNK, CHUNK), BF16),
               pltpu.VMEM((heads, N_DIR, n_chunk, LANES), F32),
               pltpu.VMEM((heads, N_DIR, seq, HEAD_DIM), F32),
               pltpu.VMEM((heads, N_DIR, HEAD_DIM, HEAD_DIM), F32)]
    res = pl.pallas_call(
        functools.partial(_delta_kernel, seq=seq, heads=heads, unroll=unroll,
                          has_state_in=s0 is not None, has_state_out=want_state),
        grid=(b, N_HEADS // heads),
        in_specs=in_specs,
        out_specs=out_specs,
        out_shape=out_shape,
        scratch_shapes=scratch,
        compiler_params=pltpu.CompilerParams(dimension_semantics=("arbitrary", "arbitrary"),
                                             vmem_limit_bytes=VMEM_LIMIT),
    )(*args)
    return (res[0], res[1]) if want_state else (res[0], None)


def _outproj_kernel(yf_ref, yd_ref, w_ref, x_ref, gate_ref, fnw_ref, o_ref, *, final):
    y = jnp.dot(yf_ref[...], w_ref[0, :D_FOURIER, :], preferred_element_type=F32)
    y = y + jnp.dot(yd_ref[...], w_ref[0, D_FOURIER:, :], preferred_element_type=F32)
    x_new = x_ref[...] + gate_ref[0] * y
    if final:
        ms = jnp.mean(x_new * x_new, axis=-1, keepdims=True)
        x_new = x_new * lax.rsqrt(ms + EPS) * fnw_ref[...]
    o_ref[...] = x_new


def _out_projection(yf2, yd2, w_out, layer, x2, mods, mod_row, final_norm_w, seq_len, final):
    m = x2.shape[0]
    tm = min(512, seq_len)
    assert m % tm == 0 and seq_len % tm == 0
    if mod_row is None:
        row = lambda i: (i * tm) // seq_len
    else:
        row = lambda i: mod_row
    return pl.pallas_call(
        functools.partial(_outproj_kernel, final=final),
        grid=(m // tm,),
        in_specs=[pl.BlockSpec((tm, D_FOURIER), lambda i: (i, 0)),
                  pl.BlockSpec((tm, D_DELTA), lambda i: (i, 0)),
                  pl.BlockSpec((1, D_MODEL, D_MODEL), lambda i: (layer, 0, 0)),
                  pl.BlockSpec((tm, D_MODEL), lambda i: (i, 0)),
                  pl.BlockSpec((1, 1, D_MODEL), lambda i: (row(i), 0, 2)),
                  pl.BlockSpec((1, D_MODEL), lambda i: (0, 0))],
        out_specs=pl.BlockSpec((tm, D_MODEL), lambda i: (i, 0)),
        out_shape=jax.ShapeDtypeStruct((m, D_MODEL), F32),
        compiler_params=pltpu.CompilerParams(dimension_semantics=("arbitrary",),
                                             vmem_limit_bytes=VMEM_LIMIT),
    )(yf2, yd2, w_out, x2, mods, final_norm_w)


def _mixer_layer(x2, batch, seq, mods, mod_row, norm_w, w_in, layer, wab, wabt, gate_params, conv_w,
                 gnorm_w, w_out, dft, s0, want_state, final_norm_w, final):
    alog_r, dt_r, alog_c, dt_c = gate_params
    proj, gi, gr, git = _in_projection(x2, mods, mod_row, norm_w, w_in, layer, wab, wabt,
                                       alog_r, dt_r, alog_c, dt_c, seq)
    proj3 = proj.reshape(batch, seq, D_MAIN)
    yf = _fourier_mixer(proj3, *dft)
    yd, state = _delta_mixer(proj3, conv_w, gi.reshape(batch, seq, LANES), gr.reshape(batch, seq, LANES),
                             git.reshape(LANES, batch, seq // CHUNK, LANES), gnorm_w, s0, layer, want_state,
                             heads=2 if seq > 4 * CHUNK else 4, unroll=4 if seq > 4 * CHUNK else 2)
    x_new = _out_projection(yf.reshape(batch * seq, D_FOURIER), yd.reshape(batch * seq, D_DELTA),
                            w_out, layer, x2, mods, mod_row, final_norm_w, seq, final)
    return x_new, state


def kernel(x_prompt, x_sample, state_ctx, c, c_ctx, norm_w, w_mod, b_mod, w_in, conv_w, a_log, dt_bias,
           gnorm_w, w_out, final_norm_w):
    depth = w_in.shape[0]
    bp, lp, _ = x_prompt.shape
    bs, ls, _ = x_sample.shape
    ctx_row = bs

    cvec = jnp.zeros((8, D_MODEL), F32).at[:bs].set(c).at[ctx_row].set(c_ctx)
    mods = _modulation(cvec, w_mod, b_mod)
    dft_ctx = _dft_tables(lp, False)
    dft_lat = _dft_tables(ls, True)
    fnw = final_norm_w.reshape(1, D_MODEL)
    w_in_t = jnp.swapaxes(w_in, 1, 2)
    w_in_bf = w_in_t.astype(BF16)
    w_gate_t = jnp.pad(w_in_t[:, D_MAIN:, :], ((0, 0), (0, LANES - N_GATE), (0, 0)))
    wabt_all = w_gate_t.astype(BF16)
    wab_all = jnp.swapaxes(w_gate_t, 1, 2).astype(BF16)
    w_out_bf = w_out.astype(BF16)

    xp = x_prompt.reshape(bp * lp, D_MODEL)
    xs = x_sample.reshape(bs * ls, D_MODEL)
    states = []
    for i in range(depth):
        wab, wabt = wab_all[i], wabt_all[i]
        pad = (0, LANES - N_DIR * N_HEADS)
        alog = jnp.pad(a_log[i].reshape(-1), pad)
        dtb = jnp.pad(dt_bias[i].reshape(-1), pad)
        gate_params = (alog.reshape(1, LANES), dtb.reshape(1, LANES),
                       alog.reshape(LANES, 1), dtb.reshape(LANES, 1))
        mods_i = mods[i].reshape(8, 1, 3 * D_MODEL)
        nw = norm_w[i].reshape(1, D_MODEL)
        gnw = gnorm_w[i].reshape(1, HEAD_DIM)
        final = i == depth - 1
        common = (nw, w_in_bf, i, wab, wabt, gate_params, conv_w[i], gnw, w_out_bf)
        xp, s_new = _mixer_layer(xp, bp, lp, mods_i, ctx_row, *common, dft_ctx, None, True, fnw, final)
        states.append(s_new)
        xs, _ = _mixer_layer(xs, bs, ls, mods_i, None, *common, dft_lat, state_ctx, False, fnw, final)
    y_prompt = xp.reshape(bp, lp, D_MODEL)
    y_sample = xs.reshape(bs, ls, D_MODEL)
    state_new = jnp.stack(states, axis=1).astype(x_prompt.dtype)
    return (y_prompt, y_sample, state_new)
```

```python
import functools
import math

import numpy as np
import jax
import jax.numpy as jnp
from jax import lax
from jax.experimental import pallas as pl
from jax.experimental.pallas import tpu as pltpu

D_MODEL = 2048
GRID_W = 64
D_FOURIER = 512
N_FOURIER_GROUPS = 4
FOURIER_GROUP = 128
D_DELTA = 1536
HEAD_DIM = 128
N_HEADS = 12
N_DIR = 2
CONV_K = 3
CHUNK = 128
EPS = 1e-6

D_MAIN = 2 * D_FOURIER + 4 * D_DELTA
N_GATE = 2 * N_DIR * N_HEADS
LANES = 128
COL_Q = (2 * D_FOURIER) // LANES
COL_K = COL_Q + N_HEADS
COL_V = COL_K + N_HEADS
COL_ZD = COL_V + N_HEADS
VMEM_LIMIT = 56 * 1024 * 1024

BF16 = jnp.bfloat16
F32 = jnp.float32
NT_DIMS = (((1,), (1,)), ((), ()))


def _mm(a, b):
    return jnp.dot(a.astype(BF16), b.astype(BF16), preferred_element_type=F32)


def _sigmoid(x):
    return 0.5 * (1.0 + jnp.tanh(0.5 * x))


def _silu(x):
    return x * _sigmoid(x)


def _mod_kernel(c_ref, w_ref, b_ref, o_ref):
    s = _silu(c_ref[...])
    o_ref[0] = _mm(s, w_ref[0]) + b_ref[0]


def _modulation(cvec, w_mod, b_mod):
    depth, _, n = w_mod.shape
    tn = 512
    return pl.pallas_call(
        _mod_kernel,
        grid=(depth, n // tn),
        in_specs=[pl.BlockSpec((8, D_MODEL), lambda l, j: (0, 0)),
                  pl.BlockSpec((1, D_MODEL, tn), lambda l, j: (l, 0, j)),
                  pl.BlockSpec((1, 1, tn), lambda l, j: (l, 0, j))],
        out_specs=pl.BlockSpec((1, 8, tn), lambda l, j: (l, 0, j)),
        out_shape=jax.ShapeDtypeStruct((depth, 8, n), F32),
        compiler_params=pltpu.CompilerParams(dimension_semantics=("arbitrary", "arbitrary")),
    )(cvec, w_mod, b_mod.reshape(depth, 1, n))


def _chunk_scans(lg, axis):
    n = lg.shape[axis]
    pos = lax.broadcasted_iota(jnp.int32, lg.shape, axis) % CHUNK
    pre, suf = lg, lg
    s = 1
    while s < CHUNK:
        pre = pre + jnp.where(pos >= s, pltpu.roll(pre, s, axis), 0.0)
        suf = suf + jnp.where(pos < CHUNK - s, pltpu.roll(suf, n - s, axis), 0.0)
        s *= 2
    return pre, suf


def _gate_tables(ab, a_log, dt_bias):
    col = lax.broadcasted_iota(jnp.int32, ab.shape, 1)
    xa = ab + dt_bias
    softplus = jnp.maximum(xa, 0.0) + jnp.log1p(jnp.exp(-jnp.abs(xa)))
    lg = jnp.where(col < N_DIR * N_HEADS, -jnp.exp(a_log) * softplus, 0.0)
    beta = _sigmoid(ab)
    pre, suf = _chunk_scans(lg, 0)
    fwd = col < N_HEADS
    dec = col < N_DIR * N_HEADS
    incl = jnp.where(fwd, pre, jnp.where(dec, suf, jnp.where(col < N_GATE, beta, 0.0)))
    rest = jnp.where(fwd, suf - lg, jnp.where(dec, pre - lg, 0.0))
    return incl, rest


def _inproj_kernel(x_ref, shift_ref, scale_ref, nw_ref, w_ref, wab_ref, alog_ref, dt_ref,
                   proj_ref, gi_ref, gr_ref, git_ref, h_scr, *, slab):
    j = pl.program_id(1)

    @pl.when(j == 0)
    def _():
        tm = x_ref.shape[0]
        for s in range(tm // slab):
            rows = pl.ds(s * slab, slab)
            xf = x_ref[rows, :]
            ms = jnp.mean(xf * xf, axis=-1, keepdims=True)
            y = xf * lax.rsqrt(ms + EPS) * nw_ref[...]
            hb = (y * (1.0 + scale_ref[0]) + shift_ref[0]).astype(BF16)
            h_scr[rows, :] = hb
            ab = jnp.dot(hb, wab_ref[...], preferred_element_type=F32)
            gi, gr = _gate_tables(ab, alog_ref[...], dt_ref[...])
            gi_ref[rows, :] = gi
            gr_ref[rows, :] = gr
            git_ref[:, s * slab:(s + 1) * slab] = jnp.transpose(gi)

    proj_ref[...] = lax.dot_general(h_scr[...], w_ref[0], NT_DIMS, preferred_element_type=F32)


def _in_projection(x2, mods, mod_row, norm_w, w_in, layer, wab, alog, dtb, seq_len):
    m = x2.shape[0]
    tm, tn, slab = 1024, 1024, 256
    assert m % tm == 0 and D_MAIN % tn == 0 and (seq_len % tm == 0 or tm % seq_len == 0)
    if mod_row is None:
        row = lambda i: (i * tm) // seq_len
    else:
        row = lambda i: mod_row
    const = lambda i, j: (0, 0)
    return pl.pallas_call(
        functools.partial(_inproj_kernel, slab=slab),
        grid=(m // tm, D_MAIN // tn),
        in_specs=[pl.BlockSpec((tm, D_MODEL), lambda i, j: (i, 0)),
                  pl.BlockSpec((1, 1, D_MODEL), lambda i, j: (row(i), 0, 0)),
                  pl.BlockSpec((1, 1, D_MODEL), lambda i, j: (row(i), 0, 1)),
                  pl.BlockSpec((1, D_MODEL), const),
                  pl.BlockSpec((1, tn, D_MODEL), lambda i, j: (layer, j, 0)),
                  pl.BlockSpec((D_MODEL, LANES), const),
                  pl.BlockSpec((1, LANES), const),
                  pl.BlockSpec((1, LANES), const)],
        out_specs=[pl.BlockSpec((tm, tn), lambda i, j: (i, j)),
                   pl.BlockSpec((tm, LANES), lambda i, j: (i, 0)),
                   pl.BlockSpec((tm, LANES), lambda i, j: (i, 0)),
                   pl.BlockSpec((LANES, tm), lambda i, j: (0, i))],
        out_shape=[jax.ShapeDtypeStruct((m, D_MAIN), F32),
                   jax.ShapeDtypeStruct((m, LANES), F32),
                   jax.ShapeDtypeStruct((m, LANES), F32),
                   jax.ShapeDtypeStruct((LANES, m), F32)],
        scratch_shapes=[pltpu.VMEM((tm, D_MODEL), BF16)],
        compiler_params=pltpu.CompilerParams(dimension_semantics=("arbitrary", "arbitrary"),
                                             vmem_limit_bytes=VMEM_LIMIT),
    )(x2, mods, mods, norm_w, w_in, wab, alog, dtb)


def _fourier_kernel(u_ref, z_ref, cs_ref, pm_ref, o_ref, z_scr):
    t = pl.program_id(1)
    seq = u_ref.shape[1]

    @pl.when(t == 0)
    def _():
        xcs = _mm(u_ref[0], cs_ref[...])
        z_scr[0:seq, :] = xcs[:, :D_FOURIER].astype(BF16)
        z_scr[seq:2 * seq, :] = xcs[:, D_FOURIER:].astype(BF16)

    y = jnp.dot(pm_ref[...], z_scr[...], preferred_element_type=F32)
    o_ref[0] = (y * _silu(z_ref[0])).astype(BF16)


def _fourier_mixer(proj3, cs, pm):
    b, seq, _ = proj3.shape
    tl = min(seq, 512)
    return pl.pallas_call(
        _fourier_kernel,
        grid=(b, seq // tl),
        in_specs=[pl.BlockSpec((1, seq, D_FOURIER), lambda i, t: (i, 0, 0)),
                  pl.BlockSpec((1, tl, D_FOURIER), lambda i, t: (i, t, 1)),
                  pl.BlockSpec((D_FOURIER, 2 * D_FOURIER), lambda i, t: (0, 0)),
                  pl.BlockSpec((tl, 2 * seq), lambda i, t: (t, 0))],
        out_specs=pl.BlockSpec((1, tl, D_FOURIER), lambda i, t: (i, t, 0)),
        out_shape=jax.ShapeDtypeStruct((b, seq, D_FOURIER), BF16),
        scratch_shapes=[pltpu.VMEM((2 * seq, D_FOURIER), BF16)],
        compiler_params=pltpu.CompilerParams(dimension_semantics=("arbitrary", "arbitrary"),
                                             vmem_limit_bytes=VMEM_LIMIT),
    )(proj3, proj3, cs, pm)


def _dft_tables(seq, grid):
    def cs(n):
        k = np.arange(n)
        ang = 2.0 * np.pi * ((k[:, None] * k[None, :]) % n) / n
        return np.cos(ang), np.sin(ang)

    cc, sc = cs(FOURIER_GROUP)
    eye = np.eye(N_FOURIER_GROUPS)
    chan = np.concatenate([np.kron(eye, cc), np.kron(eye, sc)], axis=1) / math.sqrt(FOURIER_GROUP)
    if grid:
        cr, sr = cs(seq // GRID_W)
        cw, sw = cs(GRID_W)
        cp = np.kron(cr, cw) - np.kron(sr, sw)
        sp = np.kron(sr, cw) + np.kron(cr, sw)
    else:
        cp, sp = cs(seq)
    pos = np.concatenate([cp, -sp], axis=1) / math.sqrt(seq)
    return jnp.asarray(chan, F32).astype(BF16), jnp.asarray(pos, F32).astype(BF16)


def _delta_kernel(*refs, seq, heads, unroll, has_state_in, has_state_out):
    it = iter(refs)
    take = lambda n: [next(it) for _ in range(n)]
    q_refs, k_refs, v_refs, zd_refs = take(heads), take(heads), take(heads), take(heads)
    cwq_refs, cwk_refs, cwv_refs = take(heads), take(heads), take(heads)
    gi_ref, gr_ref = next(it), next(it)
    gt_refs = [take(N_DIR) for _ in range(heads)]
    gnw_ref = next(it)
    s0_ref = next(it) if has_state_in else None
    y_ref = next(it)
    st_ref = next(it) if has_state_out else None
    qn_s, kn_s, vv_s, u_s, w_s, qg_s, kdt_s, at_s, egl_s, o_s, s_s = take(11)

    head0 = pl.program_id(1) * heads
    n_chunk = seq // CHUNK

    row = lax.broadcasted_iota(jnp.int32, (CHUNK, HEAD_DIM), 0)

    def conv_tile(src_ref, cw_ref, r0, first, last):
        cur = src_ref[0, pl.ds(r0, CHUNK), :]
        if isinstance(r0, int):
            lo, hi = max(r0 - 8, 0), min(r0 + CHUNK, seq - 8)
        else:
            lo = pl.multiple_of(jnp.maximum(r0 - 8, 0), 8)
            hi = pl.multiple_of(jnp.minimum(r0 + CHUNK, seq - 8), 8)
        before = src_ref[0, pl.ds(lo, 8), :][7:8, :]
        after = src_ref[0, pl.ds(hi, 8), :][0:1, :]
        before = jnp.where(first, 0.0, before)
        after = jnp.where(last, 0.0, after)
        prev = jnp.where(row == 0, before, pltpu.roll(cur, 1, 0))
        nxt = jnp.where(row == CHUNK - 1, after, pltpu.roll(cur, CHUNK - 1, 0))
        cw = cw_ref[...]
        y = prev * cw[0:1, :] + cur * cw[1:2, :] + nxt * cw[2:3, :]
        return _silu(y)

    def l2n(x):
        return x * lax.rsqrt(jnp.sum(x * x, axis=-1, keepdims=True) + EPS)

    def prep(m):
        r0 = m * CHUNK if isinstance(m, int) else pl.multiple_of(m * CHUNK, CHUNK)
        first, last = m == 0, m == n_chunk - 1
        rows = pl.ds(r0, CHUNK)
        for g in range(heads):
            qn_s[g, rows, :] = l2n(conv_tile(q_refs[g], cwq_refs[g], r0, first, last)) * (HEAD_DIM ** -0.5)
            kn_s[g, rows, :] = l2n(conv_tile(k_refs[g], cwk_refs[g], r0, first, last))
            vv_s[g, rows, :] = conv_tile(v_refs[g], cwv_refs[g], r0, first, last)

    ri = lax.broadcasted_iota(jnp.int32, (CHUNK, CHUNK), 0)
    ci = lax.broadcasted_iota(jnp.int32, (CHUNK, CHUNK), 1)
    eye = (ri == ci).astype(F32)
    lane = lax.broadcasted_iota(jnp.int32, (CHUNK, LANES), 1)

    def pick(tile, c):
        return jnp.sum(jnp.where(lane == c, tile, 0.0), axis=-1, keepdims=True)

    def local(t):
        tiles = []
        for g in range(heads):
            for k in range(unroll):
                m = t * unroll + k
                rows = pl.ds(m * CHUNK if isinstance(m, int) else pl.multiple_of(m * CHUNK, CHUNK), CHUNK)
                qn, kn, vv = qn_s[g, rows, :], kn_s[g, rows, :], vv_s[g, rows, :]
                kb16 = kn.astype(BF16)
                gram = lax.dot_general(jnp.concatenate([kb16, qn.astype(BF16)], axis=0), kb16, NT_DIMS,
                                       preferred_element_type=F32)
                tiles.append((g, m, rows, qn, kn, vv, gram[:CHUNK], gram[CHUNK:]))
        chains = []
        for g, m, rows, qn, kn, vv, kk, qk in tiles:
            gi_t, gr_t = gi_ref[0, rows, :], gr_ref[0, rows, :]
            for d in range(N_DIR):
                c_g = head0 + g + d * N_HEADS
                g_col = pick(gi_t, c_g)
                b_col = pick(gi_t, c_g + N_DIR * N_HEADS)
                r_col = pick(gr_t, c_g)
                g_row = gt_refs[g][d][0, 0, pl.ds(m, 1), :]
                incl = (ri >= ci) if d == 0 else (ri <= ci)
                decay = jnp.where(incl, jnp.exp(jnp.where(incl, g_col - g_row, 0.0)), 0.0)
                lmat = jnp.where(ri == ci, 0.0, b_col * kk * decay)
                at_s[g, d, m] = (qk * decay).astype(BF16)
                e_g = jnp.exp(g_col)
                qg_s[g, d, rows, :] = (qn * e_g).astype(BF16)
                kdt_s[g, d, m] = jnp.transpose(kn * jnp.exp(r_col)).astype(BF16)
                total = g_col[0:1, :] + r_col[0:1, :]
                egl_s[g, d, pl.ds(m, 1), :] = jnp.broadcast_to(jnp.exp(total), (1, LANES))
                rhs = jnp.concatenate([vv * b_col, kn * b_col * e_g], axis=1).astype(BF16)
                chains.append((g, d, rows, lmat, rhs))
        tinvs = [eye - lmat for _, _, _, lmat, _ in chains]
        powers = [lmat.astype(BF16) for _, _, _, lmat, _ in chains]
        powers = [jnp.dot(p, p, preferred_element_type=F32).astype(BF16) for p in powers]
        for _ in range(5):
            prods = [jnp.dot(jnp.concatenate([ti.astype(BF16), p], axis=0), p, preferred_element_type=F32)
                     for ti, p in zip(tinvs, powers)]
            tinvs = [ti + pr[:CHUNK] for ti, pr in zip(tinvs, prods)]
            powers = [pr[CHUNK:].astype(BF16) for pr in prods]
        tinvs = [ti + jnp.dot(ti.astype(BF16), p, preferred_element_type=F32) for ti, p in zip(tinvs, powers)]
        uws = [jnp.dot(ti.astype(BF16), rhs, preferred_element_type=F32)
               for ti, (_, _, _, _, rhs) in zip(tinvs, chains)]
        for uw, (g, d, rows, _, _) in zip(uws, chains):
            u_s[g, d, rows, :] = uw[:, :HEAD_DIM]
            w_s[g, d, rows, :] = uw[:, HEAD_DIM:].astype(BF16)

    n_iter = n_chunk // unroll
    for k in range(unroll):
        prep(k)

    def local_and_next_prep(t, carry):
        for k in range(unroll):
            prep((t + 1) * unroll + k)
        local(t)
        return carry

    lax.fori_loop(0, n_iter - 1, local_and_next_prep, 0)
    local(n_iter - 1)

    for g in range(heads):
        for d in range(N_DIR):
            s_s[g, d] = s0_ref[0, 0, d, g] if has_state_in else jnp.zeros((HEAD_DIM, HEAD_DIM), F32)

    def step(n):
        chains = [(g, d) for g in range(heads) for d in range(N_DIR)]
        where = []
        for g, d in chains:
            c = n if d == 0 else n_chunk - 1 - n
            where.append((c, pl.ds(pl.multiple_of(c * CHUNK, CHUNK), CHUNK)))
        s_old = [s_s[g, d] for g, d in chains]
        sws = [jnp.dot(jnp.concatenate([w_s[g, d, rows, :], qg_s[g, d, rows, :]], axis=0),
                       so.astype(BF16), preferred_element_type=F32)
               for (g, d), (c, rows), so in zip(chains, where, s_old)]
        vns = [(u_s[g, d, rows, :] - sw[:CHUNK]).astype(BF16)
               for (g, d), (c, rows), sw in zip(chains, where, sws)]
        outs = [jnp.dot(jnp.concatenate([at_s[g, d, c], kdt_s[g, d, c]], axis=0), vn,
                        preferred_element_type=F32)
                for (g, d), (c, rows), vn in zip(chains, where, vns)]
        for (g, d), (c, rows), so, sw, out in zip(chains, where, s_old, sws, outs):
            o_s[g, d, rows, :] = sw[CHUNK:] + out[:CHUNK]
            s_s[g, d] = so * egl_s[g, d, pl.ds(c, 1), :] + out[CHUNK:]

    def finish(c):
        rows = pl.ds(c * CHUNK if isinstance(c, int) else pl.multiple_of(c * CHUNK, CHUNK), CHUNK)
        for g in range(heads):
            o = o_s[g, 0, rows, :] + o_s[g, 1, rows, :]
            y = o * lax.rsqrt(jnp.mean(o * o, axis=-1, keepdims=True) + EPS) * gnw_ref[...]
            y_ref[0, rows, g * HEAD_DIM:(g + 1) * HEAD_DIM] = (y * _silu(zd_refs[g][0, rows, :])).astype(BF16)

    half = n_chunk // 2 + 1

    def plain_step(n, carry):
        step(n)
        return carry

    def step_and_finish(n, carry):
        finish(n - 1)
        finish(n_chunk - n)
        step(n)
        return carry

    lax.fori_loop(0, half, plain_step, 0)
    lax.fori_loop(half, n_chunk, step_and_finish, 0)
    finish(n_chunk - 1)
    finish(0)

    if has_state_out:
        for g in range(heads):
            for d in range(N_DIR):
                st_ref[0, d, g] = s_s[g, d]


def _delta_mixer(proj3, conv_w, gi3, gr3, git4, gnorm_w, s0, layer, want_state, heads, unroll):
    b, seq, _ = proj3.shape
    n_chunk = seq // CHUNK
    assert N_HEADS % heads == 0 and n_chunk % unroll == 0
    slots = range(heads)
    tok = lambda col0: [pl.BlockSpec((1, seq, HEAD_DIM), lambda i, h, g=g: (i, 0, col0 + h * heads + g))
                        for g in slots]
    cw = lambda col0: [pl.BlockSpec((CONV_K, HEAD_DIM), lambda i, h, g=g: (0, col0 + h * heads + g))
                       for g in slots]
    gate_rows = [pl.BlockSpec((1, 1, n_chunk, LANES), lambda i, h, g=g, d=d: (d * N_HEADS + h * heads + g, i, 0, 0))
                 for g in slots for d in range(N_DIR)]
    in_specs = (tok(COL_Q) + tok(COL_K) + tok(COL_V) + tok(COL_ZD) + cw(0) + cw(N_HEADS) + cw(2 * N_HEADS)
                + [pl.BlockSpec((1, seq, LANES), lambda i, h: (i, 0, 0)),
                   pl.BlockSpec((1, seq, LANES), lambda i, h: (i, 0, 0))]
                + gate_rows + [pl.BlockSpec((1, HEAD_DIM), lambda i, h: (0, 0))])
    args = [proj3] * (4 * heads) + [conv_w] * (3 * heads) + [gi3, gr3] + [git4] * (N_DIR * heads) + [gnorm_w]
    state_spec = pl.BlockSpec((1, N_DIR, heads, HEAD_DIM, HEAD_DIM), lambda i, h: (i, 0, h, 0, 0))
    if s0 is not None:
        in_specs.append(pl.BlockSpec((1, 1, N_DIR, heads, HEAD_DIM, HEAD_DIM),
                                     lambda i, h: (i, layer, 0, h, 0, 0)))
        args.append(s0)
    out_specs = [pl.BlockSpec((1, seq, heads * HEAD_DIM), lambda i, h: (i, 0, h))]
    out_shape = [jax.ShapeDtypeStruct((b, seq, D_DELTA), BF16)]
    if want_state:
        out_specs.append(state_spec)
        out_shape.append(jax.ShapeDtypeStruct((b, N_DIR, N_HEADS, HEAD_DIM, HEAD_DIM), F32))
    scratch = [pltpu.VMEM((heads, seq, HEAD_DIM), F32),
               pltpu.VMEM((heads, seq, HEAD_DIM), F32),
               pltpu.VMEM((heads, seq, HEAD_DIM), F32),
               pltpu.VMEM((heads, N_DIR, seq, HEAD_DIM), F32),
               pltpu.VMEM((heads, N_DIR, seq, HEAD_DIM), BF16),
               pltpu.VMEM((heads, N_DIR, seq, HEAD_DIM), BF16),
               pltpu.VMEM((heads, N_DIR, n_chunk, HEAD_DIM, CHUNK), BF16),
               pltpu.VMEM((heads, N_DIR, n_chunk, CHUNK, CHUNK), BF16),
               pltpu.VMEM((heads, N_DIR, n_chunk, LANES), F32),
               pltpu.VMEM((heads, N_DIR, seq, HEAD_DIM), F32),
               pltpu.VMEM((heads, N_DIR, HEAD_DIM, HEAD_DIM), F32)]
    res = pl.pallas_call(
        functools.partial(_delta_kernel, seq=seq, heads=heads, unroll=unroll,
                          has_state_in=s0 is not None, has_state_out=want_state),
        grid=(b, N_HEADS // heads),
        in_specs=in_specs,
        out_specs=out_specs,
        out_shape=out_shape,
        scratch_shapes=scratch,
        compiler_params=pltpu.CompilerParams(dimension_semantics=("arbitrary", "arbitrary"),
                                             vmem_limit_bytes=VMEM_LIMIT),
    )(*args)
    return (res[0], res[1]) if want_state else (res[0], None)


def _outproj_kernel(yf_ref, yd_ref, w_ref, x_ref, gate_ref, fnw_ref, o_ref, *, final):
    y = jnp.dot(yf_ref[...], w_ref[0, :D_FOURIER, :], preferred_element_type=F32)
    y = y + jnp.dot(yd_ref[...], w_ref[0, D_FOURIER:, :], preferred_element_type=F32)
    x_new = x_ref[...] + gate_ref[0] * y
    if final:
        ms = jnp.mean(x_new * x_new, axis=-1, keepdims=True)
        x_new = x_new * lax.rsqrt(ms + EPS) * fnw_ref[...]
    o_ref[...] = x_new


def _out_projection(yf2, yd2, w_out, layer, x2, mods, mod_row, final_norm_w, seq_len, final):
    m = x2.shape[0]
    tm = min(512, seq_len)
    assert m % tm == 0 and seq_len % tm == 0
    if mod_row is None:
        row = lambda i: (i * tm) // seq_len
    else:
        row = lambda i: mod_row
    return pl.pallas_call(
        functools.partial(_outproj_kernel, final=final),
        grid=(m // tm,),
        in_specs=[pl.BlockSpec((tm, D_FOURIER), lambda i: (i, 0)),
                  pl.BlockSpec((tm, D_DELTA), lambda i: (i, 0)),
                  pl.BlockSpec((1, D_MODEL, D_MODEL), lambda i: (layer, 0, 0)),
                  pl.BlockSpec((tm, D_MODEL), lambda i: (i, 0)),
                  pl.BlockSpec((1, 1, D_MODEL), lambda i: (row(i), 0, 2)),
                  pl.BlockSpec((1, D_MODEL), lambda i: (0, 0))],
        out_specs=pl.BlockSpec((tm, D_MODEL), lambda i: (i, 0)),
        out_shape=jax.ShapeDtypeStruct((m, D_MODEL), F32),
        compiler_params=pltpu.CompilerParams(dimension_semantics=("arbitrary",),
                                             vmem_limit_bytes=VMEM_LIMIT),
    )(yf2, yd2, w_out, x2, mods, final_norm_w)


def _mixer_layer(x2, batch, seq, mods, mod_row, norm_w, w_in, layer, wab, alog, dtb, conv_w,
                 gnorm_w, w_out, dft, s0, want_state, final_norm_w, final):
    proj, gi, gr, git = _in_projection(x2, mods, mod_row, norm_w, w_in, layer, wab, alog, dtb, seq)
    proj3 = proj.reshape(batch, seq, D_MAIN)
    yf = _fourier_mixer(proj3, *dft)
    yd, state = _delta_mixer(proj3, conv_w, gi.reshape(batch, seq, LANES), gr.reshape(batch, seq, LANES),
                             git.reshape(LANES, batch, seq // CHUNK, LANES), gnorm_w, s0, layer, want_state,
                             heads=2 if seq > 4 * CHUNK else 4, unroll=4 if seq > 4 * CHUNK else 2)
    x_new = _out_projection(yf.reshape(batch * seq, D_FOURIER), yd.reshape(batch * seq, D_DELTA),
                            w_out, layer, x2, mods, mod_row, final_norm_w, seq, final)
    return x_new, state


def kernel(x_prompt, x_sample, state_ctx, c, c_ctx, norm_w, w_mod, b_mod, w_in, conv_w, a_log, dt_bias,
           gnorm_w, w_out, final_norm_w):
    depth = w_in.shape[0]
    bp, lp, _ = x_prompt.shape
    bs, ls, _ = x_sample.shape
    ctx_row = bs

    cvec = jnp.zeros((8, D_MODEL), F32).at[:bs].set(c).at[ctx_row].set(c_ctx)
    mods = _modulation(cvec, w_mod, b_mod)
    dft_ctx = _dft_tables(lp, False)
    dft_lat = _dft_tables(ls, True)
    fnw = final_norm_w.reshape(1, D_MODEL)
    w_in_t = jnp.swapaxes(w_in, 1, 2)
    w_in_bf = w_in_t.astype(BF16)
    w_gate_t = jnp.pad(w_in_t[:, D_MAIN:, :], ((0, 0), (0, LANES - N_GATE), (0, 0)))
    wab_all = jnp.swapaxes(w_gate_t, 1, 2).astype(BF16)
    w_out_bf = w_out.astype(BF16)

    xp = x_prompt.reshape(bp * lp, D_MODEL)
    xs = x_sample.reshape(bs * ls, D_MODEL)
    states = []
    for i in range(depth):
        pad = (0, LANES - N_DIR * N_HEADS)
        alog = jnp.pad(a_log[i].reshape(-1), pad).reshape(1, LANES)
        dtb = jnp.pad(dt_bias[i].reshape(-1), pad).reshape(1, LANES)
        mods_i = mods[i].reshape(8, 1, 3 * D_MODEL)
        nw = norm_w[i].reshape(1, D_MODEL)
        gnw = gnorm_w[i].reshape(1, HEAD_DIM)
        final = i == depth - 1
        common = (nw, w_in_bf, i, wab_all[i], alog, dtb, conv_w[i], gnw, w_out_bf)
        xp, s_new = _mixer_layer(xp, bp, lp, mods_i, ctx_row, *common, dft_ctx, None, True, fnw, final)
        states.append(s_new)
        xs, _ = _mixer_layer(xs, bs, ls, mods_i, None, *common, dft_lat, state_ctx, False, fnw, final)
    y_prompt = xp.reshape(bp, lp, D_MODEL)
    y_sample = xs.reshape(bs, ls, D_MODEL)
    state_new = jnp.stack(states, axis=1).astype(x_prompt.dtype)
    return (y_prompt, y_sample, state_new)
```
